```python
import jax, jax.numpy as jnp
from jax import lax
import numpy as np

D_MODEL = 4096
BATCH = 2
SEQ = 8192
DEPTH = 2

D_MIX = D_MODEL
GROUP_DIM = 128
D_SC = D_MIX // 2
D_CF = D_MIX - D_SC
N_SC_GROUPS = D_SC // GROUP_DIM
N_CF_GROUPS = D_CF // GROUP_DIM
SC_WIDTH = 3
CF_WIDTH = 31
D_IN = 3 * D_SC + 2 * D_CF
IN_SPLITS = (D_SC, 2 * D_SC, 3 * D_SC, 3 * D_SC + D_CF)
D_FF = 11008
N_EXPERTS = 8
TOP_K = 2
D_FF_EXPERT = 4096
MOE_BLOCK = 512
N_MOD = 6
N_DENSE = (DEPTH + 1) // 2
N_MOE = DEPTH // 2
RMS_EPS = 1e-6
LN_EPS = 1e-5

kernel_name = "hybrid_conv_conformer_moe_adaln_block"


def rms_norm(x, g):
    xf = x.astype(jnp.float32)
    y = xf * lax.rsqrt(jnp.mean(xf * xf, axis=-1, keepdims=True) + RMS_EPS)
    return (y * g.astype(jnp.float32)).astype(x.dtype)


def layer_norm(x, g, b):
    xf = x.astype(jnp.float32)
    mu = jnp.mean(xf, axis=-1, keepdims=True)
    xc = xf - mu
    var = jnp.mean(xc * xc, axis=-1, keepdims=True)
    y = xc * lax.rsqrt(var + LN_EPS) * g.astype(jnp.float32) + b.astype(jnp.float32)
    return y.astype(x.dtype)


def causal_depthwise_conv(x, w):
    k = w.shape[0]
    return lax.conv_general_dilated(
        x, w.astype(x.dtype)[:, None, :], window_strides=(1,),
        padding=[(k - 1, 0)], dimension_numbers=("NWC", "WIO", "NWC"),
        feature_group_count=x.shape[-1])


def modulate(h, shift, scale):
    return h * (1.0 + scale) + shift


def hybrid_mixer(h, w_in, w_out, sc_w, cf_w, cf_b, cf_g, cf_beta):
    z = jnp.einsum("bsd,de->bse", h, w_in)
    b_gate, c_gate, v, a, g = jnp.split(z, IN_SPLITS, axis=-1)
    y_sc = b_gate * causal_depthwise_conv(c_gate * v, sc_w)
    u = a * jax.nn.sigmoid(g)
    u = causal_depthwise_conv(u, cf_w) + cf_b
    y_cf = jax.nn.silu(layer_norm(u, cf_g, cf_beta))
    y = jnp.concatenate([y_sc, y_cf], axis=-1)
    return jnp.einsum("bse,ed->bsd", y, w_out)


def swiglu(h, w_gate, w_up, w_down):
    a = jnp.einsum("bsd,df->bsf", h, w_gate)
    b = jnp.einsum("bsd,df->bsf", h, w_up)
    return jnp.einsum("bsf,fd->bsd", jax.nn.silu(a) * b, w_down)


def moe_swiglu(h, router_w, w_gate, w_up, w_down):
    bsz, seq, d = h.shape
    t = bsz * seq
    xt = h.reshape(t, d)
    logits = jnp.einsum("td,de->te", xt, router_w).astype(jnp.float32)
    top_vals, top_idx = lax.top_k(logits, TOP_K)
    gates = jax.nn.softmax(top_vals, axis=-1)
    n_assign = t * TOP_K
    e_flat = top_idx.reshape(-1).astype(jnp.int32)
    tok_flat = jnp.repeat(jnp.arange(t, dtype=jnp.int32), TOP_K)
    g_flat = gates.reshape(-1)
    order = jnp.argsort(e_flat)
    e_sorted = e_flat[order]
    tok_sorted = tok_flat[order]
    g_sorted = g_flat[order]
    counts = jnp.bincount(e_flat, length=N_EXPERTS).astype(jnp.int32)
    padded = (counts + MOE_BLOCK - 1) // MOE_BLOCK * MOE_BLOCK
    start = jnp.cumsum(counts) - counts
    pend = jnp.cumsum(padded)
    pstart = pend - padded
    rank = jnp.arange(n_assign, dtype=jnp.int32) - start[e_sorted]
    dest = pstart[e_sorted] + rank
    n_blocks = -(-n_assign // MOE_BLOCK) + N_EXPERTS
    cap = n_blocks * MOE_BLOCK
    tok_buf = jnp.full((cap,), t, jnp.int32).at[dest].set(tok_sorted)
    gate_buf = jnp.zeros((cap,), jnp.float32).at[dest].set(g_sorted)
    block_start = jnp.arange(n_blocks, dtype=jnp.int32) * MOE_BLOCK
    block_expert = jnp.minimum(
        jnp.searchsorted(pend, block_start, side="right"), N_EXPERTS - 1).astype(jnp.int32)
    x_pad = jnp.concatenate([xt, jnp.zeros((1, d), xt.dtype)], axis=0)

    def expert_block(args):
        e, tok = args
        xb = x_pad[tok]
        hb = jax.nn.silu(xb @ w_gate[e]) * (xb @ w_up[e])
        return hb @ w_down[e]

    y_buf = lax.map(expert_block, (block_expert, tok_buf.reshape(n_blocks, MOE_BLOCK)))
    y_buf = y_buf.reshape(cap, d) * gate_buf[:, None].astype(y_buf.dtype)
    out = jnp.zeros((t + 1, d), y_buf.dtype).at[tok_buf].add(y_buf)[:t]
    return out.reshape(bsz, seq, d)


def setup_inputs(seed: int = 0) -> dict:
    key = jax.random.key(seed)
    ks = jax.random.split(key, 24)
    f32 = jnp.float32

    def nrm(k, shape, scale):
        return jax.random.normal(k, shape, f32) * scale

    D = D_MODEL
    return {
        "x": nrm(ks[0], (BATCH, SEQ, D), 1.0),
        "c": nrm(ks[1], (BATCH, D), 1.0),
        "norm_mix_g": 1.0 + nrm(ks[2], (DEPTH, D), 0.05),
        "norm_ffn_g": 1.0 + nrm(ks[3], (DEPTH, D), 0.05),
        "w_ada": nrm(ks[4], (DEPTH, D, N_MOD * D), 0.5 * D ** -0.5),
        "b_ada": nrm(ks[5], (DEPTH, N_MOD * D), 0.02),
        "w_in": nrm(ks[6], (DEPTH, D, D_IN), D ** -0.5),
        "w_out": nrm(ks[7], (DEPTH, D_MIX, D), D_MIX ** -0.5),
        "sc_conv_w": nrm(ks[8], (DEPTH, SC_WIDTH, D_SC), SC_WIDTH ** -0.5),
        "cf_conv_w": nrm(ks[9], (DEPTH, CF_WIDTH, D_CF), CF_WIDTH ** -0.5),
        "cf_conv_b": nrm(ks[10], (DEPTH, D_CF), 0.02),
        "cf_ln_g": 1.0 + nrm(ks[11], (DEPTH, D_CF), 0.05),
        "cf_ln_b": nrm(ks[12], (DEPTH, D_CF), 0.02),
        "ffn_w_gate": nrm(ks[13], (N_DENSE, D, D_FF), D ** -0.5),
        "ffn_w_up": nrm(ks[14], (N_DENSE, D, D_FF), D ** -0.5),
        "ffn_w_down": nrm(ks[15], (N_DENSE, D_FF, D), D_FF ** -0.5),
        "router_w": nrm(ks[16], (N_MOE, D, N_EXPERTS), D ** -0.5),
        "moe_w_gate": nrm(ks[17], (N_MOE, N_EXPERTS, D, D_FF_EXPERT), D ** -0.5),
        "moe_w_up": nrm(ks[18], (N_MOE, N_EXPERTS, D, D_FF_EXPERT), D ** -0.5),
        "moe_w_down": nrm(ks[19], (N_MOE, N_EXPERTS, D_FF_EXPERT, D), D_FF_EXPERT ** -0.5),
        "final_g": 1.0 + nrm(ks[20], (D,), 0.05),
    }


def reference(x, c, norm_mix_g, norm_ffn_g, w_ada, b_ada, w_in, w_out, sc_conv_w,
              cf_conv_w, cf_conv_b, cf_ln_g, cf_ln_b, ffn_w_gate, ffn_w_up, ffn_w_down,
              router_w, moe_w_gate, moe_w_up, moe_w_down, final_g):
    c_act = jax.nn.silu(c)
    for l in range(DEPTH):
        mod = jnp.einsum("bd,de->be", c_act, w_ada[l]) + b_ada[l]
        sh1, sc1, g1, sh2, sc2, g2 = jnp.split(mod[:, None, :], N_MOD, axis=-1)
        h = modulate(rms_norm(x, norm_mix_g[l]), sh1, sc1)
        x = x + g1 * hybrid_mixer(h, w_in[l], w_out[l], sc_conv_w[l], cf_conv_w[l],
                                  cf_conv_b[l], cf_ln_g[l], cf_ln_b[l])
        h = modulate(rms_norm(x, norm_ffn_g[l]), sh2, sc2)
        if l % 2 == 0:
            i = l // 2
            f = swiglu(h, ffn_w_gate[i], ffn_w_up[i], ffn_w_down[i])
        else:
            i = l // 2
            f = moe_swiglu(h, router_w[i], moe_w_gate[i], moe_w_up[i], moe_w_down[i])
        x = x + g2 * f
    return rms_norm(x, final_g)
```

```python
import functools

import jax
import jax.numpy as jnp
from jax import lax
from jax.experimental import pallas as pl
from jax.experimental.pallas import tpu as pltpu

F32 = jnp.float32
BF16 = jnp.bfloat16

D_MODEL = 4096
D_SC = 2048
D_CF = 2048
D_IN = 3 * D_SC + 2 * D_CF
SC_WIDTH = 3
CF_WIDTH = 31
D_FF = 11008
D_FF_PAD = 11264
N_EXPERTS = 8
D_FF_EXPERT = 4096
MOE_BLOCK = 512
N_MOD = 6
RMS_EPS = 1e-6
LN_EPS = 1e-5

V7X_VMEM_LIMIT = 56 * 1024 * 1024
LANES = 128
MOD_ROWS = 8


def _params(n_axes, vmem=V7X_VMEM_LIMIT):
    return pltpu.CompilerParams(
        dimension_semantics=("arbitrary",) * n_axes, vmem_limit_bytes=vmem)


def _silu(v):
    return v * jax.nn.sigmoid(v)


def _mod_kernel(c_ref, w_ref, b_ref, o_ref):
    c = c_ref[...]
    acc = jnp.dot(_silu(c).astype(BF16), w_ref[0].astype(BF16),
                  preferred_element_type=F32)
    o_ref[0] = acc + b_ref[0]


def _adaln_mod(c_pad, w_ada, b_ada, tn=512):
    depth, d, n = w_ada.shape
    return pl.pallas_call(
        _mod_kernel,
        out_shape=jax.ShapeDtypeStruct((depth, MOD_ROWS, n), F32),
        grid=(depth, n // tn),
        in_specs=[
            pl.BlockSpec((MOD_ROWS, d), lambda l, j: (0, 0)),
            pl.BlockSpec((1, d, tn), lambda l, j: (l, 0, j)),
            pl.BlockSpec((1, 1, tn), lambda l, j: (l, 0, j)),
        ],
        out_specs=pl.BlockSpec((1, MOD_ROWS, tn), lambda l, j: (l, 0, j)),
        compiler_params=_params(2),
        name="adaln_mod",
    )(c_pad, w_ada, b_ada.reshape(depth, 1, n))


def _norm_mod(x, g_ref, sh_ref, sc_ref):
    ms = jnp.mean(x * x, axis=-1, keepdims=True)
    y = x * lax.rsqrt(ms + RMS_EPS) * g_ref[...]
    return y * (1.0 + sc_ref[0]) + sh_ref[0]


def _norm_mod_kernel(x_ref, g_ref, sh_ref, sc_ref, o_ref):
    o_ref[...] = _norm_mod(x_ref[...], g_ref, sh_ref, sc_ref).astype(o_ref.dtype)


def _norm_route_kernel(x_ref, g_ref, sh_ref, sc_ref, rw_ref, h_ref, info_ref):
    h = _norm_mod(x_ref[...], g_ref, sh_ref, sc_ref)
    h_ref[...] = h
    logits = jnp.dot(h, rw_ref[...], preferred_element_type=F32,
                     precision=lax.Precision.HIGHEST)
    lane = lax.broadcasted_iota(jnp.int32, logits.shape, 1)
    neg = jnp.float32(-jnp.inf)
    lg = jnp.where(lane < N_EXPERTS, logits, neg)
    m1 = jnp.max(lg, axis=-1, keepdims=True)
    i1 = jnp.min(jnp.where(lg == m1, lane, LANES), axis=-1, keepdims=True)
    lg2 = jnp.where(lane == i1, neg, lg)
    m2 = jnp.max(lg2, axis=-1, keepdims=True)
    i2 = jnp.min(jnp.where(lg2 == m2, lane, LANES), axis=-1, keepdims=True)
    e2 = jnp.exp(m2 - m1)
    den = 1.0 + e2
    info = jnp.where(lane == 0, i1.astype(F32),
           jnp.where(lane == 1, i2.astype(F32),
           jnp.where(lane == 2, 1.0 / den,
           jnp.where(lane == 3, e2 / den, 0.0))))
    info_ref[...] = info


def _row_specs(tm, d, tiles_per_batch):
    return [
        pl.BlockSpec((tm, d), lambda i: (i, 0)),
        pl.BlockSpec((1, d), lambda i: (0, 0)),
        pl.BlockSpec((1, 1, d), lambda i: (i // tiles_per_batch, 0, 0)),
        pl.BlockSpec((1, 1, d), lambda i: (i // tiles_per_batch, 0, 0)),
    ]


def _norm_modulate(x2, g, shift, scale, seq, tm=512):
    t, d = x2.shape
    return pl.pallas_call(
        _norm_mod_kernel,
        out_shape=jax.ShapeDtypeStruct((t, d), BF16),
        grid=(t // tm,),
        in_specs=_row_specs(tm, d, seq // tm),
        out_specs=pl.BlockSpec((tm, d), lambda i: (i, 0)),
        compiler_params=_params(1),
        name="norm_modulate",
    )(x2, g.reshape(1, d), shift, scale)


def _norm_route(x2, g, shift, scale, rw_pad, seq, tm=256):
    t, d = x2.shape
    return pl.pallas_call(
        _norm_route_kernel,
        out_shape=(jax.ShapeDtypeStruct((t, d), F32),
                   jax.ShapeDtypeStruct((t, LANES), F32)),
        grid=(t // tm,),
        in_specs=_row_specs(tm, d, seq // tm) + [
            pl.BlockSpec((d, LANES), lambda i: (0, 0))],
        out_specs=(pl.BlockSpec((tm, d), lambda i: (i, 0)),
                   pl.BlockSpec((tm, LANES), lambda i: (i, 0))),
        compiler_params=_params(1),
        name="norm_route",
    )(x2, g.reshape(1, d), shift, scale, rw_pad)


def _mm_kernel(a_ref, w_ref, o_ref):
    o_ref[...] = jnp.dot(a_ref[...], w_ref[...],
                         preferred_element_type=F32).astype(o_ref.dtype)


def _matmul(a, w, out_dtype, tm=1024, tn=512):
    m, k = a.shape
    n = w.shape[1]
    return pl.pallas_call(
        _mm_kernel,
        out_shape=jax.ShapeDtypeStruct((m, n), out_dtype),
        grid=(m // tm, n // tn),
        in_specs=[pl.BlockSpec((tm, k), lambda i, j: (i, 0)),
                  pl.BlockSpec((k, tn), lambda i, j: (0, j))],
        out_specs=pl.BlockSpec((tm, tn), lambda i, j: (i, j)),
        compiler_params=_params(2),
        name="matmul",
    )(a, w)


def _out_proj_kernel(a1_ref, a2_ref, w1_ref, w2_ref, x_ref, g_ref, o_ref):
    acc = jnp.dot(a1_ref[...], w1_ref[...], preferred_element_type=F32)
    acc = acc + jnp.dot(a2_ref[...], w2_ref[...], preferred_element_type=F32)
    o_ref[...] = x_ref[...] + g_ref[0] * acc


def _out_proj_residual(y_sc, y_cf, w_out, x2, gate, seq, tm=1024, tn=512):
    m, k1 = y_sc.shape
    k2 = y_cf.shape[1]
    n = w_out.shape[1]
    assert k1 == k2
    tpb = seq // tm
    return pl.pallas_call(
        _out_proj_kernel,
        out_shape=jax.ShapeDtypeStruct((m, n), F32),
        grid=(m // tm, n // tn),
        in_specs=[pl.BlockSpec((tm, k1), lambda i, j: (i, 0)),
                  pl.BlockSpec((tm, k2), lambda i, j: (i, 0)),
                  pl.BlockSpec((k1, tn), lambda i, j: (0, j)),
                  pl.BlockSpec((k2, tn), lambda i, j: (1, j)),
                  pl.BlockSpec((tm, tn), lambda i, j: (i, j)),
                  pl.BlockSpec((1, 1, tn), lambda i, j: (i // tpb, 0, j))],
        out_specs=pl.BlockSpec((tm, tn), lambda i, j: (i, j)),
        compiler_params=_params(2),
        name="out_proj_residual",
    )(y_sc, y_cf, w_out, w_out, x2, gate)


def _glu_kernel(a_ref, wg_ref, wu_ref, o_ref):
    a = a_ref[...]
    p = jnp.dot(a, wg_ref[...], preferred_element_type=F32)
    q = jnp.dot(a, wu_ref[...], preferred_element_type=F32)
    o_ref[...] = (_silu(p) * q).astype(o_ref.dtype)


def _glu(a, wg, wu, tm=1024, tn=512):
    m, k = a.shape
    n = wg.shape[1]
    return pl.pallas_call(
        _glu_kernel,
        out_shape=jax.ShapeDtypeStruct((m, n), BF16),
        grid=(m // tm, n // tn),
        in_specs=[pl.BlockSpec((tm, k), lambda i, j: (i, 0)),
                  pl.BlockSpec((k, tn), lambda i, j: (0, j)),
                  pl.BlockSpec((k, tn), lambda i, j: (0, j))],
        out_specs=pl.BlockSpec((tm, tn), lambda i, j: (i, j)),
        compiler_params=_params(2),
        name="glu",
    )(a, wg, wu)


def _down_residual_kernel(a_ref, w_ref, x_ref, g_ref, o_ref, acc_ref):
    kk = pl.program_id(2)
    part = jnp.dot(a_ref[...], w_ref[...], preferred_element_type=F32)

    @pl.when(kk == 0)
    def _():
        acc_ref[...] = part

    @pl.when(kk > 0)
    def _():
        acc_ref[...] += part

    @pl.when(kk == pl.num_programs(2) - 1)
    def _():
        o_ref[...] = x_ref[...] + g_ref[0] * acc_ref[...]


def _down_residual(a, w, x2, gate, seq, tm=1024, tn=1024, tk=2816):
    m, k = a.shape
    n = w.shape[1]
    tpb = seq // tm
    return pl.pallas_call(
        _down_residual_kernel,
        out_shape=jax.ShapeDtypeStruct((m, n), F32),
        grid=(m // tm, n // tn, k // tk),
        in_specs=[pl.BlockSpec((tm, tk), lambda i, j, kk: (i, kk)),
                  pl.BlockSpec((tk, tn), lambda i, j, kk: (kk, j)),
                  pl.BlockSpec((tm, tn), lambda i, j, kk: (i, j)),
                  pl.BlockSpec((1, 1, tn), lambda i, j, kk: (i // tpb, 0, j))],
        out_specs=pl.BlockSpec((tm, tn), lambda i, j, kk: (i, j)),
        scratch_shapes=[pltpu.VMEM((tm, tn), F32)],
        compiler_params=_params(3),
        name="down_residual",
    )(a, w, x2, gate)


SC_HALO = 8
CF_HALO = 32
MIX_TC = 512


def _mixer_kernel(b_ref, c_ref, v_ref, ch_ref, vh_ref, a_ref, g_ref, ah_ref, gh_ref,
                  scw_ref, cfw_ref, cfb_ref, lng_ref, lnb_ref,
                  ysc_ref, ycf_ref, cv_s, u_s, conv_s, *, ts, tiles_per_seq):
    i = pl.program_id(0)
    j = pl.program_id(1)
    n_chunks = pl.num_programs(1)
    seq_start = (i % tiles_per_seq) == 0

    cv_s[SC_HALO:, :] = c_ref[...] * v_ref[...]
    cv_s[0:SC_HALO, :] = jnp.where(seq_start, 0.0, ch_ref[...] * vh_ref[...])
    scw = scw_ref[...]
    y = scw[0:1] * cv_s[SC_HALO - 2:SC_HALO - 2 + ts, :]
    y = y + scw[1:2] * cv_s[SC_HALO - 1:SC_HALO - 1 + ts, :]
    y = y + scw[2:3] * cv_s[SC_HALO:SC_HALO + ts, :]
    ysc_ref[...] = (b_ref[...] * y).astype(ysc_ref.dtype)

    u_s[CF_HALO:, :] = a_ref[...] * jax.nn.sigmoid(g_ref[...])
    u_s[0:CF_HALO, :] = jnp.where(
        seq_start, 0.0, ah_ref[...] * jax.nn.sigmoid(gh_ref[...]))
    cfw = cfw_ref[...]
    acc = jnp.broadcast_to(cfb_ref[...], (ts, MIX_TC))
    base = CF_HALO - (CF_WIDTH - 1)
    for k in range(CF_WIDTH):
        acc = acc + cfw[k:k + 1] * u_s[base + k:base + k + ts, :]
    conv_s[j] = acc

    @pl.when(j == n_chunks - 1)
    def _():
        nch = D_CF // MIX_TC
        s1 = conv_s[0].sum(axis=-1, keepdims=True)
        for c in range(1, nch):
            s1 = s1 + conv_s[c].sum(axis=-1, keepdims=True)
        mu = s1 * (1.0 / D_CF)
        s2 = jnp.zeros_like(mu)
        for c in range(nch):
            dlt = conv_s[c] - mu
            s2 = s2 + (dlt * dlt).sum(axis=-1, keepdims=True)
        rs = lax.rsqrt(s2 * (1.0 / D_CF) + LN_EPS)
        for c in range(nch):
            sl = slice(c * MIX_TC, (c + 1) * MIX_TC)
            yn = (conv_s[c] - mu) * rs * lng_ref[:, sl] + lnb_ref[:, sl]
            ycf_ref[:, sl] = _silu(yn).astype(ycf_ref.dtype)


def _mixers(z, sc_w, cf_w, cf_b, ln_g, ln_b, seq, ts=256):
    t = z.shape[0]
    tc = MIX_TC
    nch = D_SC // tc
    assert D_SC == D_CF
    sc_rb = ts // SC_HALO
    cf_rb = ts // CF_HALO

    def col(off):
        return lambda i, j: (i, off * nch + j)

    def halo(off, rb):
        return lambda i, j: (jnp.maximum(i * rb - 1, 0), off * nch + j)

    kern = functools.partial(_mixer_kernel, ts=ts, tiles_per_seq=seq // ts)
    return pl.pallas_call(
        kern,
        out_shape=(jax.ShapeDtypeStruct((t, D_SC), BF16),
                   jax.ShapeDtypeStruct((t, D_CF), BF16)),
        grid=(t // ts, nch),
        in_specs=[
            pl.BlockSpec((ts, tc), col(0)),
            pl.BlockSpec((ts, tc), col(1)),
            pl.BlockSpec((ts, tc), col(2)),
            pl.BlockSpec((SC_HALO, tc), halo(1, sc_rb)),
            pl.BlockSpec((SC_HALO, tc), halo(2, sc_rb)),
            pl.BlockSpec((ts, tc), col(3)),
            pl.BlockSpec((ts, tc), col(4)),
            pl.BlockSpec((CF_HALO, tc), halo(3, cf_rb)),
            pl.BlockSpec((CF_HALO, tc), halo(4, cf_rb)),
            pl.BlockSpec((SC_WIDTH, tc), lambda i, j: (0, j)),
            pl.BlockSpec((CF_WIDTH, tc), lambda i, j: (0, j)),
            pl.BlockSpec((1, tc), lambda i, j: (0, j)),
            pl.BlockSpec((1, D_CF), lambda i, j: (0, 0)),
            pl.BlockSpec((1, D_CF), lambda i, j: (0, 0)),
        ],
        out_specs=(pl.BlockSpec((ts, tc), lambda i, j: (i, j)),
                   pl.BlockSpec((ts, D_CF), lambda i, j: (i, 0))),
        scratch_shapes=[pltpu.VMEM((ts + SC_HALO, tc), F32),
                        pltpu.VMEM((ts + CF_HALO, tc), F32),
                        pltpu.VMEM((nch, ts, tc), F32)],
        compiler_params=_params(2),
        name="token_mixers",
    )(z, z, z, z, z, z, z, z, z, sc_w, cf_w, cf_b.reshape(1, D_CF),
      ln_g.reshape(1, D_CF), ln_b.reshape(1, D_CF))


def _row_copy(src_hbm, row, dst, r, sem):
    return pltpu.make_async_copy(src_hbm.at[pl.ds(row, 1)], dst.at[pl.ds(r, 1)], sem)


def _dispatch_kernel(tok_ref, h_hbm, o_ref, buf, sem, *, rows):
    i = pl.program_id(0)
    n = pl.num_programs(0)

    def issue(blk, slot):
        def body(r, carry):
            _row_copy(h_hbm, tok_ref[blk * rows + r], buf.at[slot], r, sem.at[slot]).start()
            return carry
        lax.fori_loop(0, rows, body, 0)

    @pl.when(i == 0)
    def _():
        issue(0, 0)

    @pl.when(i + 1 < n)
    def _():
        issue(i + 1, (i + 1) % 2)

    slot = i % 2

    def wait_body(r, carry):
        _row_copy(h_hbm, 0, buf.at[slot], r, sem.at[slot]).wait()
        return carry
    lax.fori_loop(0, rows, wait_body, 0)
    o_ref[...] = buf[slot].astype(o_ref.dtype)


def _dispatch(tok_buf, h, rows=MOE_BLOCK):
    cap = tok_buf.shape[0]
    d = h.shape[1]
    return pl.pallas_call(
        functools.partial(_dispatch_kernel, rows=rows),
        out_shape=jax.ShapeDtypeStruct((cap, d), BF16),
        grid_spec=pltpu.PrefetchScalarGridSpec(
            num_scalar_prefetch=1,
            grid=(cap // rows,),
            in_specs=[pl.BlockSpec(memory_space=pl.ANY)],
            out_specs=pl.BlockSpec((rows, d), lambda i, tok: (i, 0)),
            scratch_shapes=[pltpu.VMEM((2, rows, d), F32),
                            pltpu.SemaphoreType.DMA((2,))]),
        compiler_params=_params(1),
        name="moe_dispatch",
    )(tok_buf, h)


def _moe_glu_kernel(blk_ref, tile_ref, exp_ref, live_ref, x_ref, wg_ref, wu_ref, o_ref):
    s = pl.program_id(0)

    @pl.when(live_ref[s] == 1)
    def _():
        x = x_ref[...]
        p = jnp.dot(x, wg_ref[0], preferred_element_type=F32)
        q = jnp.dot(x, wu_ref[0], preferred_element_type=F32)
        o_ref[...] = (_silu(p) * q).astype(o_ref.dtype)

    @pl.when(live_ref[s] == 0)
    def _():
        o_ref[...] = jnp.zeros_like(o_ref)


def _moe_down_kernel(blk_ref, tile_ref, exp_ref, live_ref, a_ref, w_ref, o_ref):
    s = pl.program_id(0)

    @pl.when(live_ref[s] == 1)
    def _():
        o_ref[...] = jnp.dot(a_ref[...], w_ref[0], preferred_element_type=F32)

    @pl.when(live_ref[s] == 0)
    def _():
        o_ref[...] = jnp.zeros_like(o_ref)


def _grouped_schedule(block_start, block_count, n_blocks, n_tiles):
    step_end = (block_start + block_count) * n_tiles
    s = jnp.arange(n_blocks * n_tiles, dtype=jnp.int32)
    e = jnp.minimum(jnp.searchsorted(step_end, s, side="right"),
                    N_EXPERTS - 1).astype(jnp.int32)
    local = s - block_start[e] * n_tiles
    nb = jnp.maximum(block_count[e], 1)
    tile = local // nb
    blk = block_start[e] + local % nb
    return blk.astype(jnp.int32), tile.astype(jnp.int32), e


def _moe_glu(sched, xg, wg, wu, tn=512):
    cap, d = xg.shape
    n = wg.shape[2]
    steps = sched[0].shape[0]
    return pl.pallas_call(
        _moe_glu_kernel,
        out_shape=jax.ShapeDtypeStruct((cap, n), BF16),
        grid_spec=pltpu.PrefetchScalarGridSpec(
            num_scalar_prefetch=4,
            grid=(steps,),
            in_specs=[
                pl.BlockSpec((MOE_BLOCK, d), lambda s, b, t, e, v: (b[s], 0)),
                pl.BlockSpec((1, d, tn), lambda s, b, t, e, v: (e[s], 0, t[s])),
                pl.BlockSpec((1, d, tn), lambda s, b, t, e, v: (e[s], 0, t[s])),
            ],
            out_specs=pl.BlockSpec((MOE_BLOCK, tn), lambda s, b, t, e, v: (b[s], t[s]))),
        compiler_params=_params(1),
        name="moe_glu",
    )(*sched, xg, wg, wu)


def _moe_down(sched, a, wd, tn=512):
    cap, k = a.shape
    n = wd.shape[2]
    steps = sched[0].shape[0]
    return pl.pallas_call(
        _moe_down_kernel,
        out_shape=jax.ShapeDtypeStruct((cap, n), F32),
        grid_spec=pltpu.PrefetchScalarGridSpec(
            num_scalar_prefetch=4,
            grid=(steps,),
            in_specs=[
                pl.BlockSpec((MOE_BLOCK, k), lambda s, b, t, e, v: (b[s], 0)),
                pl.BlockSpec((1, k, tn), lambda s, b, t, e, v: (e[s], 0, t[s])),
            ],
            out_specs=pl.BlockSpec((MOE_BLOCK, tn), lambda s, b, t, e, v: (b[s], t[s]))),
        compiler_params=_params(1),
        name="moe_down",
    )(*sched, a, wd)


def _combine_kernel(pos_ref, x_ref, info_ref, g_ref, fg_ref, y_hbm, o_ref, buf, sem,
                    *, rows, n_tok):
    i = pl.program_id(0)
    n = pl.num_programs(0)

    def issue(blk, slot):
        def body(r, carry):
            for k in range(2):
                row = pos_ref[k * n_tok + blk * rows + r]
                _row_copy(y_hbm, row, buf.at[slot, k], r, sem.at[slot]).start()
            return carry
        lax.fori_loop(0, rows, body, 0)

    @pl.when(i == 0)
    def _():
        issue(0, 0)

    @pl.when(i + 1 < n)
    def _():
        issue(i + 1, (i + 1) % 2)

    slot = i % 2

    def wait_body(r, carry):
        for k in range(2):
            _row_copy(y_hbm, 0, buf.at[slot, k], r, sem.at[slot]).wait()
        return carry
    lax.fori_loop(0, rows, wait_body, 0)

    info = info_ref[...]
    f = info[:, 2:3] * buf[slot, 0] + info[:, 3:4] * buf[slot, 1]
    xn = x_ref[...] + g_ref[0] * f
    ms = jnp.mean(xn * xn, axis=-1, keepdims=True)
    o_ref[...] = xn * lax.rsqrt(ms + RMS_EPS) * fg_ref[...]


def _combine_final_norm(pos, x2, info, gate, final_g, y, seq, rows=256):
    t, d = x2.shape
    tpb = seq // rows
    return pl.pallas_call(
        functools.partial(_combine_kernel, rows=rows, n_tok=t),
        out_shape=jax.ShapeDtypeStruct((t, d), F32),
        grid_spec=pltpu.PrefetchScalarGridSpec(
            num_scalar_prefetch=1,
            grid=(t // rows,),
            in_specs=[
                pl.BlockSpec((rows, d), lambda i, p: (i, 0)),
                pl.BlockSpec((rows, LANES), lambda i, p: (i, 0)),
                pl.BlockSpec((1, 1, d), lambda i, p: (i // tpb, 0, 0)),
                pl.BlockSpec((1, d), lambda i, p: (0, 0)),
                pl.BlockSpec(memory_space=pl.ANY),
            ],
            out_specs=pl.BlockSpec((rows, d), lambda i, p: (i, 0)),
            scratch_shapes=[pltpu.VMEM((2, 2, rows, d), F32),
                            pltpu.SemaphoreType.DMA((2,))]),
        compiler_params=_params(1),
        name="moe_combine_final_norm",
    )(pos, x2, info, gate, final_g.reshape(1, d), y)


def _moe_layer_and_final_norm(x2, g, shift, scale, gate, router_w, wg, wu, wd,
                              final_g, seq):
    t, d = x2.shape
    rw_pad = jnp.pad(router_w, ((0, 0), (0, LANES - N_EXPERTS)))
    h, info = _norm_route(x2, g, shift, scale, rw_pad, seq)

    e_flat = info[:, 0:2].astype(jnp.int32).reshape(-1)
    n_assign = e_flat.shape[0]
    onehot = (e_flat[:, None] == jnp.arange(N_EXPERTS, dtype=jnp.int32)).astype(jnp.int32)
    csum = jnp.cumsum(onehot, axis=0)
    rank = jnp.sum((csum - onehot) * onehot, axis=1)
    counts = csum[-1]
    padded = (counts + MOE_BLOCK - 1) // MOE_BLOCK * MOE_BLOCK
    pend = jnp.cumsum(padded)
    pstart = pend - padded
    dest = (pstart[e_flat] + rank).astype(jnp.int32)
    n_blocks = -(-n_assign // MOE_BLOCK) + N_EXPERTS
    cap = n_blocks * MOE_BLOCK
    tok_flat = jnp.arange(n_assign, dtype=jnp.int32) // 2
    tok_buf = jnp.full((cap,), t - 1, jnp.int32).at[dest].set(tok_flat)

    block_start = (pstart // MOE_BLOCK).astype(jnp.int32)
    block_count = (padded // MOE_BLOCK).astype(jnp.int32)
    used_blocks = pend[-1] // MOE_BLOCK
    block_count = block_count.at[N_EXPERTS - 1].set(n_blocks - block_start[N_EXPERTS - 1])

    xg = _dispatch(tok_buf, h)

    def sched(n_tiles):
        blk, tile, e = _grouped_schedule(block_start, block_count, n_blocks, n_tiles)
        return blk, tile, e, (blk < used_blocks).astype(jnp.int32)

    tn = 512
    hmid = _moe_glu(sched(wg.shape[2] // tn), xg, wg, wu, tn=tn)
    y = _moe_down(sched(wd.shape[2] // tn), hmid, wd, tn=tn)

    pos = dest.reshape(t, 2).T.reshape(-1)
    return _combine_final_norm(pos, x2, info, gate, final_g, y, seq)


def kernel(x, c, norm_mix_g, norm_ffn_g, w_ada, b_ada, w_in, w_out, sc_conv_w, cf_conv_w,
           cf_conv_b, cf_ln_g, cf_ln_b, ffn_w_gate, ffn_w_up, ffn_w_down, router_w,
           moe_w_gate, moe_w_up, moe_w_down, final_g):
    bsz, seq, d = x.shape
    depth = w_ada.shape[0]
    assert depth == 2 and d == D_MODEL
    t = bsz * seq
    x2 = x.reshape(t, d)

    c_pad = jnp.pad(c, ((0, MOD_ROWS - bsz), (0, 0)))
    mod = _adaln_mod(c_pad, w_ada, b_ada)
    mod = mod[:, :bsz].reshape(depth, bsz, N_MOD, 1, d)

    ff_pad = D_FF_PAD - D_FF
    out = None
    for l in range(depth):
        sh1, sc1, g1, sh2, sc2, g2 = (mod[l, :, m] for m in range(N_MOD))

        h = _norm_modulate(x2, norm_mix_g[l], sh1, sc1, seq)
        z = _matmul(h, w_in[l].astype(BF16), F32)
        y_sc, y_cf = _mixers(z, sc_conv_w[l], cf_conv_w[l], cf_conv_b[l],
                             cf_ln_g[l], cf_ln_b[l], seq)
        x2 = _out_proj_residual(y_sc, y_cf, w_out[l].astype(BF16), x2, g1, seq)

        i = l // 2
        if l % 2 == 0:
            h = _norm_modulate(x2, norm_ffn_g[l], sh2, sc2, seq)
            wg = jnp.pad(ffn_w_gate[i].astype(BF16), ((0, 0), (0, ff_pad)))
            wu = jnp.pad(ffn_w_up[i].astype(BF16), ((0, 0), (0, ff_pad)))
            wd = jnp.pad(ffn_w_down[i].astype(BF16), ((0, ff_pad), (0, 0)))
            hmid = _glu(h, wg, wu)
            x2 = _down_residual(hmid, wd, x2, g2, seq)
        else:
            out = _moe_layer_and_final_norm(
                x2, norm_ffn_g[l], sh2, sc2, g2, router_w[i],
                moe_w_gate[i].astype(BF16), moe_w_up[i].astype(BF16),
                moe_w_down[i].astype(BF16), final_g, seq)
    return out.reshape(bsz, seq, d)
```

```python
import functools

import jax
import jax.numpy as jnp
from jax import lax
from jax.experimental import pallas as pl
from jax.experimental.pallas import tpu as pltpu

F32 = jnp.float32
BF16 = jnp.bfloat16

D_MODEL = 4096
D_SC = 2048
D_CF = 2048
D_IN = 3 * D_SC + 2 * D_CF
SC_WIDTH = 3
CF_WIDTH = 31
D_FF = 11008
D_FF_PAD = 11264
N_EXPERTS = 8
D_FF_EXPERT = 4096
MOE_BLOCK = 512
N_MOD = 6
RMS_EPS = 1e-6
LN_EPS = 1e-5

V7X_VMEM_LIMIT = 56 * 1024 * 1024
LANES = 128
SUBLANES = 8
MOD_ROWS = 8


def _params(n_axes, vmem=V7X_VMEM_LIMIT):
    return pltpu.CompilerParams(
        dimension_semantics=("arbitrary",) * n_axes, vmem_limit_bytes=vmem)


def _silu(v):
    return v * jax.nn.sigmoid(v)


def _mod_kernel(c_ref, w_ref, b_ref, o_ref):
    c = c_ref[...]
    acc = jnp.dot(_silu(c).astype(BF16), w_ref[0].astype(BF16),
                  preferred_element_type=F32)
    o_ref[0] = acc + b_ref[0]


def _adaln_mod(c_pad, w_ada, b_ada, tn=512):
    depth, d, n = w_ada.shape
    return pl.pallas_call(
        _mod_kernel,
        out_shape=jax.ShapeDtypeStruct((depth, MOD_ROWS, n), F32),
        grid=(depth, n // tn),
        in_specs=[
            pl.BlockSpec((MOD_ROWS, d), lambda l, j: (0, 0)),
            pl.BlockSpec((1, d, tn), lambda l, j: (l, 0, j)),
            pl.BlockSpec((1, 1, tn), lambda l, j: (l, 0, j)),
        ],
        out_specs=pl.BlockSpec((1, MOD_ROWS, tn), lambda l, j: (l, 0, j)),
        compiler_params=_params(2),
        name="adaln_mod",
    )(c_pad, w_ada, b_ada.reshape(depth, 1, n))


def _norm_mod(x, g_ref, sh_ref, sc_ref):
    ms = jnp.mean(x * x, axis=-1, keepdims=True)
    y = x * lax.rsqrt(ms + RMS_EPS) * g_ref[...]
    return y * (1.0 + sc_ref[0]) + sh_ref[0]


def _norm_mod_kernel(x_ref, g_ref, sh_ref, sc_ref, o_ref):
    o_ref[...] = _norm_mod(x_ref[...], g_ref, sh_ref, sc_ref).astype(o_ref.dtype)


def _pack_bf16_pair(lo, hi):
    def rounded_bits(v):
        b = pltpu.bitcast(v, jnp.uint32)
        return b + jnp.uint32(0x7FFF) + ((b >> 16) & jnp.uint32(1))
    return (rounded_bits(hi) & jnp.uint32(0xFFFF0000)) | (rounded_bits(lo) >> 16)


def _unpack_bf16_pair(w):
    lo = pltpu.bitcast(w << 16, F32).astype(BF16)
    hi = pltpu.bitcast(w & jnp.uint32(0xFFFF0000), F32).astype(BF16)
    return lo, hi


def _norm_route_kernel(x_ref, g_ref, sh_ref, sc_ref, rw_ref, hp_ref, info_ref):
    h = _norm_mod(x_ref[...], g_ref, sh_ref, sc_ref)
    half = h.shape[1] // 2
    hp_ref[...] = _pack_bf16_pair(h[:, :half], h[:, half:])
    rw = rw_ref[...]
    h_hi = h.astype(BF16)
    h_lo = (h - h_hi.astype(F32)).astype(BF16)
    rw_hi = rw.astype(BF16)
    rw_lo = (rw - rw_hi.astype(F32)).astype(BF16)
    logits = (jnp.dot(h_hi, rw_hi, preferred_element_type=F32)
              + jnp.dot(h_lo, rw_hi, preferred_element_type=F32)
              + jnp.dot(h_hi, rw_lo, preferred_element_type=F32))
    lane = lax.broadcasted_iota(jnp.int32, logits.shape, 1)
    neg = jnp.float32(-jnp.inf)
    lg = jnp.where(lane < N_EXPERTS, logits, neg)
    m1 = jnp.max(lg, axis=-1, keepdims=True)
    i1 = jnp.min(jnp.where(lg == m1, lane, LANES), axis=-1, keepdims=True)
    lg2 = jnp.where(lane == i1, neg, lg)
    m2 = jnp.max(lg2, axis=-1, keepdims=True)
    i2 = jnp.min(jnp.where(lg2 == m2, lane, LANES), axis=-1, keepdims=True)
    e2 = jnp.exp(m2 - m1)
    den = 1.0 + e2
    info = jnp.where(lane == 0, i1.astype(F32),
           jnp.where(lane == 1, i2.astype(F32),
           jnp.where(lane == 2, 1.0 / den,
           jnp.where(lane == 3, e2 / den, 0.0))))
    info_ref[...] = info


def _row_specs(tm, d, tiles_per_batch):
    return [
        pl.BlockSpec((tm, d), lambda i: (i, 0)),
        pl.BlockSpec((1, d), lambda i: (0, 0)),
        pl.BlockSpec((1, 1, d), lambda i: (i // tiles_per_batch, 0, 0)),
        pl.BlockSpec((1, 1, d), lambda i: (i // tiles_per_batch, 0, 0)),
    ]


def _norm_modulate(x2, g, shift, scale, seq, tm=512):
    t, d = x2.shape
    return pl.pallas_call(
        _norm_mod_kernel,
        out_shape=jax.ShapeDtypeStruct((t, d), BF16),
        grid=(t // tm,),
        in_specs=_row_specs(tm, d, seq // tm),
        out_specs=pl.BlockSpec((tm, d), lambda i: (i, 0)),
        compiler_params=_params(1),
        name="norm_modulate",
    )(x2, g.reshape(1, d), shift, scale)


def _norm_route(x2, g, shift, scale, rw_pad, seq, tm=256):
    t, d = x2.shape
    return pl.pallas_call(
        _norm_route_kernel,
        out_shape=(jax.ShapeDtypeStruct((t, d // 2), jnp.uint32),
                   jax.ShapeDtypeStruct((t, LANES), F32)),
        grid=(t // tm,),
        in_specs=_row_specs(tm, d, seq // tm) + [
            pl.BlockSpec((d, LANES), lambda i: (0, 0))],
        out_specs=(pl.BlockSpec((tm, d // 2), lambda i: (i, 0)),
                   pl.BlockSpec((tm, LANES), lambda i: (i, 0))),
        compiler_params=_params(1),
        name="norm_route",
    )(x2, g.reshape(1, d), shift, scale, rw_pad)


def _mm_kernel(a_ref, w_ref, o_ref):
    o_ref[...] = jnp.dot(a_ref[...], w_ref[...],
                         preferred_element_type=F32).astype(o_ref.dtype)


def _matmul(a, w, out_dtype, tm=1024, tn=512):
    m, k = a.shape
    n = w.shape[1]
    return pl.pallas_call(
        _mm_kernel,
        out_shape=jax.ShapeDtypeStruct((m, n), out_dtype),
        grid=(m // tm, n // tn),
        in_specs=[pl.BlockSpec((tm, k), lambda i, j: (i, 0)),
                  pl.BlockSpec((k, tn), lambda i, j: (0, j))],
        out_specs=pl.BlockSpec((tm, tn), lambda i, j: (i, j)),
        compiler_params=_params(2),
        name="matmul",
    )(a, w)


def _out_proj_kernel(a1_ref, a2_ref, w1_ref, w2_ref, x_ref, g_ref, o_ref):
    acc = jnp.dot(a1_ref[...], w1_ref[...], preferred_element_type=F32)
    acc = acc + jnp.dot(a2_ref[...], w2_ref[...], preferred_element_type=F32)
    o_ref[...] = x_ref[...] + g_ref[0] * acc


def _out_proj_residual(y_sc, y_cf, w_out, x2, gate, seq, tm=1024, tn=512):
    m, k1 = y_sc.shape
    k2 = y_cf.shape[1]
    n = w_out.shape[1]
    assert k1 == k2
    tpb = seq // tm
    return pl.pallas_call(
        _out_proj_kernel,
        out_shape=jax.ShapeDtypeStruct((m, n), F32),
        grid=(m // tm, n // tn),
        in_specs=[pl.BlockSpec((tm, k1), lambda i, j: (i, 0)),
                  pl.BlockSpec((tm, k2), lambda i, j: (i, 0)),
                  pl.BlockSpec((k1, tn), lambda i, j: (0, j)),
                  pl.BlockSpec((k2, tn), lambda i, j: (1, j)),
                  pl.BlockSpec((tm, tn), lambda i, j: (i, j)),
                  pl.BlockSpec((1, 1, tn), lambda i, j: (i // tpb, 0, j))],
        out_specs=pl.BlockSpec((tm, tn), lambda i, j: (i, j)),
        compiler_params=_params(2),
        name="out_proj_residual",
    )(y_sc, y_cf, w_out, w_out, x2, gate)


def _glu_kernel(a_ref, wg_ref, wu_ref, o_ref):
    a = a_ref[...]
    p = jnp.dot(a, wg_ref[...], preferred_element_type=F32)
    q = jnp.dot(a, wu_ref[...], preferred_element_type=F32)
    o_ref[...] = (_silu(p) * q).astype(o_ref.dtype)


def _glu(a, wg, wu, tm=1024, tn=512):
    m, k = a.shape
    n = wg.shape[1]
    return pl.pallas_call(
        _glu_kernel,
        out_shape=jax.ShapeDtypeStruct((m, n), BF16),
        grid=(m // tm, n // tn),
        in_specs=[pl.BlockSpec((tm, k), lambda i, j: (i, 0)),
                  pl.BlockSpec((k, tn), lambda i, j: (0, j)),
                  pl.BlockSpec((k, tn), lambda i, j: (0, j))],
        out_specs=pl.BlockSpec((tm, tn), lambda i, j: (i, j)),
        compiler_params=_params(2),
        name="glu",
    )(a, wg, wu)


def _down_residual_kernel(a_ref, w_ref, x_ref, g_ref, o_ref, acc_ref):
    kk = pl.program_id(2)
    part = jnp.dot(a_ref[...], w_ref[...], preferred_element_type=F32)

    @pl.when(kk == 0)
    def _():
        acc_ref[...] = part

    @pl.when(kk > 0)
    def _():
        acc_ref[...] += part

    @pl.when(kk == pl.num_programs(2) - 1)
    def _():
        o_ref[...] = x_ref[...] + g_ref[0] * acc_ref[...]


def _down_residual(a, w, x2, gate, seq, tm=1024, tn=1024, tk=2816):
    m, k = a.shape
    n = w.shape[1]
    tpb = seq // tm
    return pl.pallas_call(
        _down_residual_kernel,
        out_shape=jax.ShapeDtypeStruct((m, n), F32),
        grid=(m // tm, n // tn, k // tk),
        in_specs=[pl.BlockSpec((tm, tk), lambda i, j, kk: (i, kk)),
                  pl.BlockSpec((tk, tn), lambda i, j, kk: (kk, j)),
                  pl.BlockSpec((tm, tn), lambda i, j, kk: (i, j)),
                  pl.BlockSpec((1, 1, tn), lambda i, j, kk: (i // tpb, 0, j))],
        out_specs=pl.BlockSpec((tm, tn), lambda i, j, kk: (i, j)),
        scratch_shapes=[pltpu.VMEM((tm, tn), F32)],
        compiler_params=_params(3),
        name="down_residual",
    )(a, w, x2, gate)


SC_HALO = SUBLANES
CF_HALO = 32
MIX_TC = 512


def _mixer_kernel(b_ref, c_ref, v_ref, ch_ref, vh_ref, a_ref, g_ref, ah_ref, gh_ref,
                  scw_ref, cfw_ref, cfb_ref, lng_ref, lnb_ref,
                  ysc_ref, ycf_ref, cv_s, u_s, sh_s, conv_s, *, ts, tiles_per_seq):
    i = pl.program_id(0)
    j = pl.program_id(1)
    n_chunks = pl.num_programs(1)
    seq_start = (i % tiles_per_seq) == 0

    cv_s[SC_HALO:, :] = c_ref[...] * v_ref[...]
    cv_s[0:SC_HALO, :] = jnp.where(seq_start, 0.0, ch_ref[...] * vh_ref[...])
    scw = scw_ref[...]
    y = scw[0:1] * cv_s[SC_HALO - 2:SC_HALO - 2 + ts, :]
    y = y + scw[1:2] * cv_s[SC_HALO - 1:SC_HALO - 1 + ts, :]
    y = y + scw[2:3] * cv_s[SC_HALO:SC_HALO + ts, :]
    ysc_ref[...] = (b_ref[...] * y).astype(ysc_ref.dtype)

    u_s[CF_HALO:, :] = a_ref[...] * jax.nn.sigmoid(g_ref[...])
    u_s[0:CF_HALO, :] = jnp.where(
        seq_start, 0.0, ah_ref[...] * jax.nn.sigmoid(gh_ref[...]))
    span = ts + CF_HALO - SUBLANES
    for r in range(1, SUBLANES):
        sh_s[r - 1] = u_s[r:r + span, :]
    cfw = cfw_ref[...]
    acc = jnp.broadcast_to(cfb_ref[...], (ts, MIX_TC))
    base = CF_HALO - (CF_WIDTH - 1)
    for k in range(CF_WIDTH):
        q, r = divmod(base + k, SUBLANES)
        if r == 0:
            tap = u_s[q * SUBLANES:q * SUBLANES + ts, :]
        else:
            tap = sh_s[r - 1, q * SUBLANES:q * SUBLANES + ts, :]
        acc = acc + cfw[k:k + 1] * tap
    conv_s[j] = acc

    @pl.when(j == n_chunks - 1)
    def _():
        nch = D_CF // MIX_TC
        s1 = conv_s[0].sum(axis=-1, keepdims=True)
        for c in range(1, nch):
            s1 = s1 + conv_s[c].sum(axis=-1, keepdims=True)
        mu = s1 * (1.0 / D_CF)
        s2 = jnp.zeros_like(mu)
        for c in range(nch):
            dlt = conv_s[c] - mu
            s2 = s2 + (dlt * dlt).sum(axis=-1, keepdims=True)
        rs = lax.rsqrt(s2 * (1.0 / D_CF) + LN_EPS)
        for c in range(nch):
            sl = slice(c * MIX_TC, (c + 1) * MIX_TC)
            yn = (conv_s[c] - mu) * rs * lng_ref[:, sl] + lnb_ref[:, sl]
            ycf_ref[:, sl] = _silu(yn).astype(ycf_ref.dtype)


def _mixers(z, sc_w, cf_w, cf_b, ln_g, ln_b, seq, ts=256):
    t = z.shape[0]
    tc = MIX_TC
    nch = D_SC // tc
    assert D_SC == D_CF
    sc_rb = ts // SC_HALO
    cf_rb = ts // CF_HALO

    def col(off):
        return lambda i, j: (i, off * nch + j)

    def halo(off, rb):
        return lambda i, j: (jnp.maximum(i * rb - 1, 0), off * nch + j)

    kern = functools.partial(_mixer_kernel, ts=ts, tiles_per_seq=seq // ts)
    return pl.pallas_call(
        kern,
        out_shape=(jax.ShapeDtypeStruct((t, D_SC), BF16),
                   jax.ShapeDtypeStruct((t, D_CF), BF16)),
        grid=(t // ts, nch),
        in_specs=[
            pl.BlockSpec((ts, tc), col(0)),
            pl.BlockSpec((ts, tc), col(1)),
            pl.BlockSpec((ts, tc), col(2)),
            pl.BlockSpec((SC_HALO, tc), halo(1, sc_rb)),
            pl.BlockSpec((SC_HALO, tc), halo(2, sc_rb)),
            pl.BlockSpec((ts, tc), col(3)),
            pl.BlockSpec((ts, tc), col(4)),
            pl.BlockSpec((CF_HALO, tc), halo(3, cf_rb)),
            pl.BlockSpec((CF_HALO, tc), halo(4, cf_rb)),
            pl.BlockSpec((SC_WIDTH, tc), lambda i, j: (0, j)),
            pl.BlockSpec((CF_WIDTH, tc), lambda i, j: (0, j)),
            pl.BlockSpec((1, tc), lambda i, j: (0, j)),
            pl.BlockSpec((1, D_CF), lambda i, j: (0, 0)),
            pl.BlockSpec((1, D_CF), lambda i, j: (0, 0)),
        ],
        out_specs=(pl.BlockSpec((ts, tc), lambda i, j: (i, j)),
                   pl.BlockSpec((ts, D_CF), lambda i, j: (i, 0))),
        scratch_shapes=[pltpu.VMEM((ts + SC_HALO, tc), F32),
                        pltpu.VMEM((ts + CF_HALO, tc), F32),
                        pltpu.VMEM((SUBLANES - 1, ts + CF_HALO - SUBLANES, tc), F32),
                        pltpu.VMEM((nch, ts, tc), F32)],
        compiler_params=_params(2),
        name="token_mixers",
    )(z, z, z, z, z, z, z, z, z, sc_w, cf_w, cf_b.reshape(1, D_CF),
      ln_g.reshape(1, D_CF), ln_b.reshape(1, D_CF))


def _row_copy(src_hbm, row, dst, r, sem):
    return pltpu.make_async_copy(src_hbm.at[pl.ds(row, 1)], dst.at[pl.ds(r, 1)], sem)


DISPATCH_CHUNK = 512


def _dispatch_kernel(tok_ref, src_hbm, dst_hbm, sem, *, n_rows):
    n_chunks = n_rows // DISPATCH_CHUNK

    def copy(p, row, slot):
        return pltpu.make_async_copy(
            src_hbm.at[pl.ds(row, 1)], dst_hbm.at[pl.ds(p, 1)], sem.at[slot])

    def issue(ci):
        def body(r, carry):
            p = ci * DISPATCH_CHUNK + r
            copy(p, tok_ref[p], ci % 2).start()
            return carry
        lax.fori_loop(0, DISPATCH_CHUNK, body, 0, unroll=8)

    def drain(ci):
        def body(r, carry):
            copy(ci * DISPATCH_CHUNK + r, 0, ci % 2).wait()
            return carry
        lax.fori_loop(0, DISPATCH_CHUNK, body, 0, unroll=8)

    issue(0)

    def step(ci, carry):
        issue(ci)
        drain(ci - 1)
        return carry
    lax.fori_loop(1, n_chunks, step, 0)
    drain(n_chunks - 1)


def _dispatch(tok_buf, hp):
    cap = tok_buf.shape[0]
    assert cap % DISPATCH_CHUNK == 0
    return pl.pallas_call(
        functools.partial(_dispatch_kernel, n_rows=cap),
        out_shape=jax.ShapeDtypeStruct((cap, hp.shape[1]), hp.dtype),
        in_specs=[pl.BlockSpec(memory_space=pltpu.SMEM),
                  pl.BlockSpec(memory_space=pl.ANY)],
        out_specs=pl.BlockSpec(memory_space=pl.ANY),
        scratch_shapes=[pltpu.SemaphoreType.DMA((2,))],
        name="moe_dispatch",
    )(tok_buf, hp)


def _weight_tile_copy(w_hbm, e, t, stage, sem):
    tn = stage.shape[1]
    col = pl.multiple_of(t * tn, tn)
    return pltpu.make_async_copy(w_hbm.at[e, :, pl.ds(col, tn)], stage, sem)


def _refresh_weights(s, tile_ref, exp_ref, first_ref, next_ref, w_hbms, stages, w_bf16s, sem):
    def copies(step):
        return [_weight_tile_copy(w, exp_ref[step], tile_ref[step], st, sem.at[n])
                for n, (w, st) in enumerate(zip(w_hbms, stages))]

    @pl.when(s == 0)
    def _():
        for c in copies(0):
            c.start()

    @pl.when(first_ref[s] == 1)
    def _():
        for c in copies(s):
            c.wait()
        for st, wb in zip(stages, w_bf16s):
            wb[...] = st[...].astype(BF16)
        nxt = next_ref[s]

        @pl.when(nxt >= 0)
        def _():
            for c in copies(nxt):
                c.start()


def _moe_glu_kernel(blk_ref, tile_ref, exp_ref, live_ref, first_ref, next_ref,
                    x_ref, wg_hbm, wu_hbm, o_ref, stage_g, stage_u, wg_s, wu_s, sem):
    s = pl.program_id(0)
    _refresh_weights(s, tile_ref, exp_ref, first_ref, next_ref,
                     (wg_hbm, wu_hbm), (stage_g, stage_u), (wg_s, wu_s), sem)

    @pl.when(live_ref[s] == 1)
    def _():
        x = jnp.concatenate(_unpack_bf16_pair(x_ref[...]), axis=-1)
        p = jnp.dot(x, wg_s[...], preferred_element_type=F32)
        q = jnp.dot(x, wu_s[...], preferred_element_type=F32)
        o_ref[...] = (_silu(p) * q).astype(o_ref.dtype)

    @pl.when(live_ref[s] == 0)
    def _():
        o_ref[...] = jnp.zeros_like(o_ref)


def _moe_down_kernel(blk_ref, tile_ref, exp_ref, live_ref, first_ref, next_ref,
                     a_ref, w_hbm, o_ref, stage, w_s, sem):
    s = pl.program_id(0)
    _refresh_weights(s, tile_ref, exp_ref, first_ref, next_ref,
                     (w_hbm,), (stage,), (w_s,), sem)

    @pl.when(live_ref[s] == 1)
    def _():
        o_ref[...] = jnp.dot(a_ref[...], w_s[...], preferred_element_type=F32)

    @pl.when(live_ref[s] == 0)
    def _():
        o_ref[...] = jnp.zeros_like(o_ref)


def _grouped_schedule(block_start, block_count, used_blocks, n_blocks, n_tiles):
    steps = n_blocks * n_tiles
    step_end = (block_start + block_count) * n_tiles
    s = jnp.arange(steps, dtype=jnp.int32)
    e = jnp.minimum(jnp.searchsorted(step_end, s, side="right"),
                    N_EXPERTS - 1).astype(jnp.int32)
    local = s - block_start[e] * n_tiles
    nb = jnp.maximum(block_count[e], 1)
    tile = (local // nb).astype(jnp.int32)
    row = local % nb
    blk = (block_start[e] + row).astype(jnp.int32)
    live = (blk < used_blocks).astype(jnp.int32)
    first = (row == 0).astype(jnp.int32)
    nxt = s + nb
    nxt = jnp.where(nxt < steps, nxt, -1).astype(jnp.int32)
    return blk, tile, e, live, first, nxt


def _moe_glu(sched, xg, wg, wu, tn=512):
    cap, half = xg.shape
    _, d, n = wg.shape
    assert d == 2 * half
    steps = sched[0].shape[0]
    return pl.pallas_call(
        _moe_glu_kernel,
        out_shape=jax.ShapeDtypeStruct((cap, n), BF16),
        grid_spec=pltpu.PrefetchScalarGridSpec(
            num_scalar_prefetch=len(sched),
            grid=(steps,),
            in_specs=[
                pl.BlockSpec((MOE_BLOCK, half), lambda s, b, *_: (b[s], 0)),
                pl.BlockSpec(memory_space=pl.ANY),
                pl.BlockSpec(memory_space=pl.ANY),
            ],
            out_specs=pl.BlockSpec((MOE_BLOCK, tn), lambda s, b, t, *_: (b[s], t[s])),
            scratch_shapes=[pltpu.VMEM((d, tn), F32), pltpu.VMEM((d, tn), F32),
                            pltpu.VMEM((d, tn), BF16), pltpu.VMEM((d, tn), BF16),
                            pltpu.SemaphoreType.DMA((2,))]),
        compiler_params=_params(1),
        name="moe_glu",
    )(*sched, xg, wg, wu)


def _moe_down(sched, a, wd, tn=1024):
    cap, k = a.shape
    n = wd.shape[2]
    steps = sched[0].shape[0]
    return pl.pallas_call(
        _moe_down_kernel,
        out_shape=jax.ShapeDtypeStruct((cap, n), F32),
        grid_spec=pltpu.PrefetchScalarGridSpec(
            num_scalar_prefetch=len(sched),
            grid=(steps,),
            in_specs=[
                pl.BlockSpec((MOE_BLOCK, k), lambda s, b, *_: (b[s], 0)),
                pl.BlockSpec(memory_space=pl.ANY),
            ],
            out_specs=pl.BlockSpec((MOE_BLOCK, tn), lambda s, b, t, *_: (b[s], t[s])),
            scratch_shapes=[pltpu.VMEM((k, tn), F32), pltpu.VMEM((k, tn), BF16),
                            pltpu.SemaphoreType.DMA((1,))]),
        compiler_params=_params(1),
        name="moe_down",
    )(*sched, a, wd)


def _combine_kernel(pos_ref, x_ref, info_ref, g_ref, fg_ref, y_hbm, o_ref, buf, sem,
                    *, rows, n_tok):
    i = pl.program_id(0)
    n = pl.num_programs(0)

    def issue(blk, slot):
        def body(r, carry):
            for k in range(2):
                row = pos_ref[k * n_tok + blk * rows + r]
                _row_copy(y_hbm, row, buf.at[slot, k], r, sem.at[slot]).start()
            return carry
        lax.fori_loop(0, rows, body, 0, unroll=4)

    @pl.when(i == 0)
    def _():
        issue(0, 0)

    @pl.when(i + 1 < n)
    def _():
        issue(i + 1, (i + 1) % 2)

    slot = i % 2

    def wait_body(r, carry):
        for k in range(2):
            _row_copy(y_hbm, 0, buf.at[slot, k], r, sem.at[slot]).wait()
        return carry
    lax.fori_loop(0, rows, wait_body, 0, unroll=4)

    info = info_ref[...]
    f = info[:, 2:3] * buf[slot, 0] + info[:, 3:4] * buf[slot, 1]
    xn = x_ref[...] + g_ref[0] * f
    ms = jnp.mean(xn * xn, axis=-1, keepdims=True)
    o_ref[...] = xn * lax.rsqrt(ms + RMS_EPS) * fg_ref[...]


def _combine_final_norm(pos, x2, info, gate, final_g, y, seq, rows=256):
    t, d = x2.shape
    tpb = seq // rows
    return pl.pallas_call(
        functools.partial(_combine_kernel, rows=rows, n_tok=t),
        out_shape=jax.ShapeDtypeStruct((t, d), F32),
        grid_spec=pltpu.PrefetchScalarGridSpec(
            num_scalar_prefetch=1,
            grid=(t // rows,),
            in_specs=[
                pl.BlockSpec((rows, d), lambda i, p: (i, 0)),
                pl.BlockSpec((rows, LANES), lambda i, p: (i, 0)),
                pl.BlockSpec((1, 1, d), lambda i, p: (i // tpb, 0, 0)),
                pl.BlockSpec((1, d), lambda i, p: (0, 0)),
                pl.BlockSpec(memory_space=pl.ANY),
            ],
            out_specs=pl.BlockSpec((rows, d), lambda i, p: (i, 0)),
            scratch_shapes=[pltpu.VMEM((2, 2, rows, d), F32),
                            pltpu.SemaphoreType.DMA((2,))]),
        compiler_params=_params(1),
        name="moe_combine_final_norm",
    )(pos, x2, info, gate, final_g.reshape(1, d), y)


def _moe_layer_and_final_norm(x2, g, shift, scale, gate, router_w, wg, wu, wd,
                              final_g, seq):
    t, d = x2.shape
    rw_pad = jnp.pad(router_w, ((0, 0), (0, LANES - N_EXPERTS)))
    hp, info = _norm_route(x2, g, shift, scale, rw_pad, seq)

    e_flat = info[:, 0:2].astype(jnp.int32).reshape(-1)
    n_assign = e_flat.shape[0]
    onehot = (e_flat[:, None] == jnp.arange(N_EXPERTS, dtype=jnp.int32)).astype(jnp.int32)
    csum = jnp.cumsum(onehot, axis=0)
    rank = jnp.sum((csum - onehot) * onehot, axis=1)
    counts = csum[-1]
    padded = (counts + MOE_BLOCK - 1) // MOE_BLOCK * MOE_BLOCK
    pend = jnp.cumsum(padded)
    pstart = pend - padded
    dest = (pstart[e_flat] + rank).astype(jnp.int32)
    n_blocks = -(-n_assign // MOE_BLOCK) + N_EXPERTS
    cap = n_blocks * MOE_BLOCK
    tok_flat = jnp.arange(n_assign, dtype=jnp.int32) // 2
    tok_buf = jnp.full((cap,), t - 1, jnp.int32).at[dest].set(tok_flat)

    block_start = (pstart // MOE_BLOCK).astype(jnp.int32)
    block_count = (padded // MOE_BLOCK).astype(jnp.int32)
    used_blocks = pend[-1] // MOE_BLOCK
    block_count = block_count.at[N_EXPERTS - 1].set(n_blocks - block_start[N_EXPERTS - 1])

    xg = _dispatch(tok_buf, hp)

    def sched(n_tiles):
        return _grouped_schedule(block_start, block_count, used_blocks, n_blocks, n_tiles)

    tn_glu, tn_down = 512, 1024
    hmid = _moe_glu(sched(wg.shape[2] // tn_glu), xg, wg, wu, tn=tn_glu)
    y = _moe_down(sched(wd.shape[2] // tn_down), hmid, wd, tn=tn_down)

    pos = dest.reshape(t, 2).T.reshape(-1)
    return _combine_final_norm(pos, x2, info, gate, final_g, y, seq)


def kernel(x, c, norm_mix_g, norm_ffn_g, w_ada, b_ada, w_in, w_out, sc_conv_w, cf_conv_w,
           cf_conv_b, cf_ln_g, cf_ln_b, ffn_w_gate, ffn_w_up, ffn_w_down, router_w,
           moe_w_gate, moe_w_up, moe_w_down, final_g):
    bsz, seq, d = x.shape
    depth = w_ada.shape[0]
    assert depth == 2 and d == D_MODEL
    t = bsz * seq
    x2 = x.reshape(t, d)

    c_pad = jnp.pad(c, ((0, MOD_ROWS - bsz), (0, 0)))
    mod = _adaln_mod(c_pad, w_ada, b_ada)
    mod = mod[:, :bsz].reshape(depth, bsz, N_MOD, 1, d)

    ff_pad = D_FF_PAD - D_FF
    out = None
    for l in range(depth):
        sh1, sc1, g1, sh2, sc2, g2 = (mod[l, :, m] for m in range(N_MOD))

        h = _norm_modulate(x2, norm_mix_g[l], sh1, sc1, seq)
        z = _matmul(h, w_in[l].astype(BF16), F32)
        y_sc, y_cf = _mixers(z, sc_conv_w[l], cf_conv_w[l], cf_conv_b[l],
                             cf_ln_g[l], cf_ln_b[l], seq)
        x2 = _out_proj_residual(y_sc, y_cf, w_out[l].astype(BF16), x2, g1, seq)

        i = l // 2
        if l % 2 == 0:
            h = _norm_modulate(x2, norm_ffn_g[l], sh2, sc2, seq)
            wg = jnp.pad(ffn_w_gate[i].astype(BF16), ((0, 0), (0, ff_pad)))
            wu = jnp.pad(ffn_w_up[i].astype(BF16), ((0, 0), (0, ff_pad)))
            wd = jnp.pad(ffn_w_down[i].astype(BF16), ((0, ff_pad), (0, 0)))
            hmid = _glu(h, wg, wu)
            x2 = _down_residual(hmid, wd, x2, g2, seq)
        else:
            out = _moe_layer_and_final_norm(
                x2, norm_ffn_g[l], sh2, sc2, g2, router_w[i],
                moe_w_gate[i], moe_w_up[i], moe_w_down[i], final_g, seq)
    return out.reshape(bsz, seq, d)
```

```python
import functools

import jax
import jax.numpy as jnp
from jax import lax
from jax.experimental import pallas as pl
from jax.experimental.pallas import tpu as pltpu

F32 = jnp.float32
BF16 = jnp.bfloat16

D_MODEL = 4096
D_SC = 2048
D_CF = 2048
D_IN = 3 * D_SC + 2 * D_CF
SC_WIDTH = 3
CF_WIDTH = 31
N_EXPERTS = 8
D_FF_EXPERT = 4096
MOE_BLOCK = 512
N_MOD = 6
RMS_EPS = 1e-6
LN_EPS = 1e-5

V7X_VMEM_LIMIT = 56 * 1024 * 1024
LANES = 128
SUBLANES = 8
MOD_ROWS = 8


def _params(n_axes, vmem=V7X_VMEM_LIMIT):
    return pltpu.CompilerParams(
        dimension_semantics=("arbitrary",) * n_axes, vmem_limit_bytes=vmem)


def _silu(v):
    return v * jax.nn.sigmoid(v)


def _mod_kernel(c_ref, w_ref, b_ref, o_ref):
    c = c_ref[...]
    acc = jnp.dot(_silu(c).astype(BF16), w_ref[0].astype(BF16),
                  preferred_element_type=F32)
    o_ref[0] = acc + b_ref[0]


def _adaln_mod(c_pad, w_ada, b_ada, tn=512):
    depth, d, n = w_ada.shape
    return pl.pallas_call(
        _mod_kernel,
        out_shape=jax.ShapeDtypeStruct((depth, MOD_ROWS, n), F32),
        grid=(depth, n // tn),
        in_specs=[
            pl.BlockSpec((MOD_ROWS, d), lambda l, j: (0, 0)),
            pl.BlockSpec((1, d, tn), lambda l, j: (l, 0, j)),
            pl.BlockSpec((1, 1, tn), lambda l, j: (l, 0, j)),
        ],
        out_specs=pl.BlockSpec((1, MOD_ROWS, tn), lambda l, j: (l, 0, j)),
        compiler_params=_params(2),
        name="adaln_mod",
    )(c_pad, w_ada, b_ada.reshape(depth, 1, n))


def _norm_mod(x, g_ref, sh_ref, sc_ref):
    ms = jnp.mean(x * x, axis=-1, keepdims=True)
    y = x * lax.rsqrt(ms + RMS_EPS) * g_ref[...]
    return y * (1.0 + sc_ref[0]) + sh_ref[0]


def _norm_mod_kernel(x_ref, g_ref, sh_ref, sc_ref, o_ref):
    o_ref[...] = _norm_mod(x_ref[...], g_ref, sh_ref, sc_ref).astype(o_ref.dtype)


def _pack_bf16_pair(lo, hi):
    def rounded_bits(v):
        b = pltpu.bitcast(v, jnp.uint32)
        return b + jnp.uint32(0x7FFF) + ((b >> 16) & jnp.uint32(1))
    return (rounded_bits(hi) & jnp.uint32(0xFFFF0000)) | (rounded_bits(lo) >> 16)


def _unpack_bf16_pair(w):
    lo = pltpu.bitcast(w << 16, F32).astype(BF16)
    hi = pltpu.bitcast(w & jnp.uint32(0xFFFF0000), F32).astype(BF16)
    return lo, hi


def _norm_route_kernel(x_ref, g_ref, sh_ref, sc_ref, rw_ref, hp_ref, info_ref):
    h = _norm_mod(x_ref[...], g_ref, sh_ref, sc_ref)
    half = h.shape[1] // 2
    hp_ref[...] = _pack_bf16_pair(h[:, :half], h[:, half:])
    rw = rw_ref[...]
    h_hi = h.astype(BF16)
    h_lo = (h - h_hi.astype(F32)).astype(BF16)
    rw_hi = rw.astype(BF16)
    rw_lo = (rw - rw_hi.astype(F32)).astype(BF16)
    logits = (jnp.dot(h_hi, rw_hi, preferred_element_type=F32)
              + jnp.dot(h_lo, rw_hi, preferred_element_type=F32)
              + jnp.dot(h_hi, rw_lo, preferred_element_type=F32))
    lane = lax.broadcasted_iota(jnp.int32, logits.shape, 1)
    neg = jnp.float32(-jnp.inf)
    lg = jnp.where(lane < N_EXPERTS, logits, neg)
    m1 = jnp.max(lg, axis=-1, keepdims=True)
    i1 = jnp.min(jnp.where(lg == m1, lane, LANES), axis=-1, keepdims=True)
    lg2 = jnp.where(lane == i1, neg, lg)
    m2 = jnp.max(lg2, axis=-1, keepdims=True)
    i2 = jnp.min(jnp.where(lg2 == m2, lane, LANES), axis=-1, keepdims=True)
    e2 = jnp.exp(m2 - m1)
    den = 1.0 + e2
    info = jnp.where(lane == 0, i1.astype(F32),
           jnp.where(lane == 1, i2.astype(F32),
           jnp.where(lane == 2, 1.0 / den,
           jnp.where(lane == 3, e2 / den, 0.0))))
    info_ref[...] = info


def _row_specs(tm, d, tiles_per_batch):
    return [
        pl.BlockSpec((tm, d), lambda i: (i, 0)),
        pl.BlockSpec((1, d), lambda i: (0, 0)),
        pl.BlockSpec((1, 1, d), lambda i: (i // tiles_per_batch, 0, 0)),
        pl.BlockSpec((1, 1, d), lambda i: (i // tiles_per_batch, 0, 0)),
    ]


def _norm_modulate(x2, g, shift, scale, seq, tm=512):
    t, d = x2.shape
    return pl.pallas_call(
        _norm_mod_kernel,
        out_shape=jax.ShapeDtypeStruct((t, d), BF16),
        grid=(t // tm,),
        in_specs=_row_specs(tm, d, seq // tm),
        out_specs=pl.BlockSpec((tm, d), lambda i: (i, 0)),
        compiler_params=_params(1),
        name="norm_modulate",
    )(x2, g.reshape(1, d), shift, scale)


def _norm_route(x2, g, shift, scale, rw_pad, seq, tm=256):
    t, d = x2.shape
    return pl.pallas_call(
        _norm_route_kernel,
        out_shape=(jax.ShapeDtypeStruct((t, d // 2), jnp.uint32),
                   jax.ShapeDtypeStruct((t, LANES), F32)),
        grid=(t // tm,),
        in_specs=_row_specs(tm, d, seq // tm) + [
            pl.BlockSpec((d, LANES), lambda i: (0, 0))],
        out_specs=(pl.BlockSpec((tm, d // 2), lambda i: (i, 0)),
                   pl.BlockSpec((tm, LANES), lambda i: (i, 0))),
        compiler_params=_params(1),
        name="norm_route",
    )(x2, g.reshape(1, d), shift, scale, rw_pad)


def _mm_kernel(a_ref, w_ref, o_ref):
    o_ref[...] = jnp.dot(a_ref[...], w_ref[0],
                         preferred_element_type=F32).astype(o_ref.dtype)


def _matmul(a, w, layer, out_dtype, tm=1024, tn=512):
    m, k = a.shape
    n = w.shape[2]
    return pl.pallas_call(
        _mm_kernel,
        out_shape=jax.ShapeDtypeStruct((m, n), out_dtype),
        grid=(m // tm, n // tn),
        in_specs=[pl.BlockSpec((tm, k), lambda i, j: (i, 0)),
                  pl.BlockSpec((1, k, tn), lambda i, j: (layer, 0, j))],
        out_specs=pl.BlockSpec((tm, tn), lambda i, j: (i, j)),
        compiler_params=_params(2),
        name="matmul",
    )(a, w)


def _out_proj_kernel(a1_ref, a2_ref, w1_ref, w2_ref, x_ref, g_ref, o_ref):
    acc = jnp.dot(a1_ref[...], w1_ref[0], preferred_element_type=F32)
    acc = acc + jnp.dot(a2_ref[...], w2_ref[0], preferred_element_type=F32)
    o_ref[...] = x_ref[...] + g_ref[0] * acc


def _out_proj_residual(y_sc, y_cf, w_out, layer, x2, gate, seq, tm=1024, tn=512):
    m, k1 = y_sc.shape
    k2 = y_cf.shape[1]
    n = w_out.shape[2]
    assert k1 == k2
    tpb = seq // tm
    return pl.pallas_call(
        _out_proj_kernel,
        out_shape=jax.ShapeDtypeStruct((m, n), F32),
        grid=(m // tm, n // tn),
        in_specs=[pl.BlockSpec((tm, k1), lambda i, j: (i, 0)),
                  pl.BlockSpec((tm, k2), lambda i, j: (i, 0)),
                  pl.BlockSpec((1, k1, tn), lambda i, j: (layer, 0, j)),
                  pl.BlockSpec((1, k2, tn), lambda i, j: (layer, 1, j)),
                  pl.BlockSpec((tm, tn), lambda i, j: (i, j)),
                  pl.BlockSpec((1, 1, tn), lambda i, j: (i // tpb, 0, j))],
        out_specs=pl.BlockSpec((tm, tn), lambda i, j: (i, j)),
        compiler_params=_params(2),
        name="out_proj_residual",
    )(y_sc, y_cf, w_out, w_out, x2, gate)


def _glu_kernel(a_ref, wg_ref, wu_ref, o_ref):
    a = a_ref[...]
    p = jnp.dot(a, wg_ref[0], preferred_element_type=F32)
    q = jnp.dot(a, wu_ref[0], preferred_element_type=F32)
    o_ref[...] = (_silu(p) * q).astype(o_ref.dtype)


def _glu(a, wg, wu, layer, tm=1024, tn=256):
    m, k = a.shape
    n = wg.shape[2]
    assert n % tn == 0
    return pl.pallas_call(
        _glu_kernel,
        out_shape=jax.ShapeDtypeStruct((m, n), BF16),
        grid=(m // tm, n // tn),
        in_specs=[pl.BlockSpec((tm, k), lambda i, j: (i, 0)),
                  pl.BlockSpec((1, k, tn), lambda i, j: (layer, 0, j)),
                  pl.BlockSpec((1, k, tn), lambda i, j: (layer, 0, j))],
        out_specs=pl.BlockSpec((tm, tn), lambda i, j: (i, j)),
        compiler_params=_params(2),
        name="glu",
    )(a, wg, wu)


def _down_residual_kernel(a_ref, w_ref, x_ref, g_ref, o_ref):
    acc = jnp.dot(a_ref[...], w_ref[0], preferred_element_type=F32)
    o_ref[...] = x_ref[...] + g_ref[0] * acc


def _down_residual(a, w, layer, x2, gate, seq, tm=512, tn=256):
    m, k = a.shape
    n = w.shape[2]
    tpb = seq // tm
    return pl.pallas_call(
        _down_residual_kernel,
        out_shape=jax.ShapeDtypeStruct((m, n), F32),
        grid=(m // tm, n // tn),
        in_specs=[pl.BlockSpec((tm, k), lambda i, j: (i, 0)),
                  pl.BlockSpec((1, k, tn), lambda i, j: (layer, 0, j)),
                  pl.BlockSpec((tm, tn), lambda i, j: (i, j)),
                  pl.BlockSpec((1, 1, tn), lambda i, j: (i // tpb, 0, j))],
        out_specs=pl.BlockSpec((tm, tn), lambda i, j: (i, j)),
        compiler_params=_params(2),
        name="down_residual",
    )(a, w, x2, gate)


SC_HALO = SUBLANES
CF_HALO = 32
MIX_TC = 512


def _mixer_kernel(b_ref, c_ref, v_ref, ch_ref, vh_ref, a_ref, g_ref, ah_ref, gh_ref,
                  scw_ref, cfw_ref, cfb_ref, lng_ref, lnb_ref,
                  ysc_ref, ycf_ref, cv_s, u_s, sh_s, conv_s, *, ts, tiles_per_seq):
    i = pl.program_id(0)
    j = pl.program_id(1)
    n_chunks = pl.num_programs(1)
    seq_start = (i % tiles_per_seq) == 0

    cv_s[SC_HALO:, :] = c_ref[...] * v_ref[...]
    cv_s[0:SC_HALO, :] = jnp.where(seq_start, 0.0, ch_ref[...] * vh_ref[...])
    scw = scw_ref[...]
    y = scw[0:1] * cv_s[SC_HALO - 2:SC_HALO - 2 + ts, :]
    y = y + scw[1:2] * cv_s[SC_HALO - 1:SC_HALO - 1 + ts, :]
    y = y + scw[2:3] * cv_s[SC_HALO:SC_HALO + ts, :]
    ysc_ref[...] = (b_ref[...] * y).astype(ysc_ref.dtype)

    u_s[CF_HALO:, :] = a_ref[...] * jax.nn.sigmoid(g_ref[...])
    u_s[0:CF_HALO, :] = jnp.where(
        seq_start, 0.0, ah_ref[...] * jax.nn.sigmoid(gh_ref[...]))
    span = ts + CF_HALO - SUBLANES
    for r in range(1, SUBLANES):
        sh_s[r - 1] = u_s[r:r + span, :]
    cfw = cfw_ref[...]
    acc = jnp.broadcast_to(cfb_ref[...], (ts, MIX_TC))
    base = CF_HALO - (CF_WIDTH - 1)
    for k in range(CF_WIDTH):
        q, r = divmod(base + k, SUBLANES)
        if r == 0:
            tap = u_s[q * SUBLANES:q * SUBLANES + ts, :]
        else:
            tap = sh_s[r - 1, q * SUBLANES:q * SUBLANES + ts, :]
        acc = acc + cfw[k:k + 1] * tap
    conv_s[j] = acc

    @pl.when(j == n_chunks - 1)
    def _():
        nch = D_CF // MIX_TC
        s1 = conv_s[0].sum(axis=-1, keepdims=True)
        for c in range(1, nch):
            s1 = s1 + conv_s[c].sum(axis=-1, keepdims=True)
        mu = s1 * (1.0 / D_CF)
        s2 = jnp.zeros_like(mu)
        for c in range(nch):
            dlt = conv_s[c] - mu
            s2 = s2 + (dlt * dlt).sum(axis=-1, keepdims=True)
        rs = lax.rsqrt(s2 * (1.0 / D_CF) + LN_EPS)
        for c in range(nch):
            sl = slice(c * MIX_TC, (c + 1) * MIX_TC)
            yn = (conv_s[c] - mu) * rs * lng_ref[:, sl] + lnb_ref[:, sl]
            ycf_ref[:, sl] = _silu(yn).astype(ycf_ref.dtype)


def _mixers(z, sc_w, cf_w, cf_b, ln_g, ln_b, seq, ts=256):
    t = z.shape[0]
    tc = MIX_TC
    nch = D_SC // tc
    assert D_SC == D_CF
    sc_rb = ts // SC_HALO
    cf_rb = ts // CF_HALO

    def col(off):
        return lambda i, j: (i, off * nch + j)

    def halo(off, rb):
        return lambda i, j: (jnp.maximum(i * rb - 1, 0), off * nch + j)

    kern = functools.partial(_mixer_kernel, ts=ts, tiles_per_seq=seq // ts)
    return pl.pallas_call(
        kern,
        out_shape=(jax.ShapeDtypeStruct((t, D_SC), BF16),
                   jax.ShapeDtypeStruct((t, D_CF), BF16)),
        grid=(t // ts, nch),
        in_specs=[
            pl.BlockSpec((ts, tc), col(0)),
            pl.BlockSpec((ts, tc), col(1)),
            pl.BlockSpec((ts, tc), col(2)),
            pl.BlockSpec((SC_HALO, tc), halo(1, sc_rb)),
            pl.BlockSpec((SC_HALO, tc), halo(2, sc_rb)),
            pl.BlockSpec((ts, tc), col(3)),
            pl.BlockSpec((ts, tc), col(4)),
            pl.BlockSpec((CF_HALO, tc), halo(3, cf_rb)),
            pl.BlockSpec((CF_HALO, tc), halo(4, cf_rb)),
            pl.BlockSpec((SC_WIDTH, tc), lambda i, j: (0, j)),
            pl.BlockSpec((CF_WIDTH, tc), lambda i, j: (0, j)),
            pl.BlockSpec((1, tc), lambda i, j: (0, j)),
            pl.BlockSpec((1, D_CF), lambda i, j: (0, 0)),
            pl.BlockSpec((1, D_CF), lambda i, j: (0, 0)),
        ],
        out_specs=(pl.BlockSpec((ts, tc), lambda i, j: (i, j)),
                   pl.BlockSpec((ts, D_CF), lambda i, j: (i, 0))),
        scratch_shapes=[pltpu.VMEM((ts + SC_HALO, tc), F32),
                        pltpu.VMEM((ts + CF_HALO, tc), F32),
                        pltpu.VMEM((SUBLANES - 1, ts + CF_HALO - SUBLANES, tc), F32),
                        pltpu.VMEM((nch, ts, tc), F32)],
        compiler_params=_params(2),
        name="token_mixers",
    )(z, z, z, z, z, z, z, z, z, sc_w, cf_w, cf_b.reshape(1, D_CF),
      ln_g.reshape(1, D_CF), ln_b.reshape(1, D_CF))


def _row_copy(src_hbm, row, dst, r, sem):
    return pltpu.make_async_copy(src_hbm.at[pl.ds(row, 1)], dst.at[pl.ds(r, 1)], sem)


def _dispatch_kernel(tok_ref, src_hbm, o_ref, sem, *, rows):
    base = pl.program_id(0) * rows

    def issue(r, carry):
        _row_copy(src_hbm, tok_ref[base + r], o_ref, r, sem).start()
        return carry
    lax.fori_loop(0, rows, issue, 0, unroll=8)

    def drain(r, carry):
        _row_copy(src_hbm, 0, o_ref, r, sem).wait()
        return carry
    lax.fori_loop(0, rows, drain, 0, unroll=8)


def _dispatch(tok_buf, hp, rows=MOE_BLOCK):
    cap = tok_buf.shape[0]
    width = hp.shape[1]
    return pl.pallas_call(
        functools.partial(_dispatch_kernel, rows=rows),
        out_shape=jax.ShapeDtypeStruct((cap, width), hp.dtype),
        grid_spec=pltpu.PrefetchScalarGridSpec(
            num_scalar_prefetch=1,
            grid=(cap // rows,),
            in_specs=[pl.BlockSpec(memory_space=pl.ANY)],
            out_specs=pl.BlockSpec((rows, width), lambda i, tok: (i, 0)),
            scratch_shapes=[pltpu.SemaphoreType.DMA]),
        compiler_params=_params(1),
        name="moe_dispatch",
    )(tok_buf, hp)


def _weight_tile_copy(w_hbm, e, t, stage, sem):
    tn = stage.shape[1]
    col = pl.multiple_of(t * tn, tn)
    return pltpu.make_async_copy(w_hbm.at[e, :, pl.ds(col, tn)], stage, sem)


def _refresh_weights(s, tile_ref, exp_ref, first_ref, next_ref, w_hbms, stages, w_bf16s, sem):
    def copies(step):
        return [_weight_tile_copy(w, exp_ref[step], tile_ref[step], st, sem.at[n])
                for n, (w, st) in enumerate(zip(w_hbms, stages))]

    @pl.when(s == 0)
    def _():
        for c in copies(0):
            c.start()

    @pl.when(first_ref[s] == 1)
    def _():
        for c in copies(s):
            c.wait()
        for st, wb in zip(stages, w_bf16s):
            wb[...] = st[...].astype(BF16)
        nxt = next_ref[s]

        @pl.when(nxt >= 0)
        def _():
            for c in copies(nxt):
                c.start()


def _moe_glu_kernel(blk_ref, tile_ref, exp_ref, live_ref, first_ref, next_ref,
                    x_ref, wg_hbm, wu_hbm, o_ref, stage_g, stage_u, wg_s, wu_s, sem):
    s = pl.program_id(0)
    _refresh_weights(s, tile_ref, exp_ref, first_ref, next_ref,
                     (wg_hbm, wu_hbm), (stage_g, stage_u), (wg_s, wu_s), sem)

    @pl.when(live_ref[s] == 1)
    def _():
        x = jnp.concatenate(_unpack_bf16_pair(x_ref[...]), axis=-1)
        p = jnp.dot(x, wg_s[...], preferred_element_type=F32)
        q = jnp.dot(x, wu_s[...], preferred_element_type=F32)
        o_ref[...] = (_silu(p) * q).astype(o_ref.dtype)

    @pl.when(live_ref[s] == 0)
    def _():
        o_ref[...] = jnp.zeros_like(o_ref)


def _moe_down_kernel(blk_ref, tile_ref, exp_ref, live_ref, first_ref, next_ref,
                     a_ref, w_hbm, o_ref, stage, w_s, sem):
    s = pl.program_id(0)
    _refresh_weights(s, tile_ref, exp_ref, first_ref, next_ref,
                     (w_hbm,), (stage,), (w_s,), sem)

    @pl.when(live_ref[s] == 1)
    def _():
        o_ref[...] = jnp.dot(a_ref[...], w_s[...], preferred_element_type=F32)

    @pl.when(live_ref[s] == 0)
    def _():
        o_ref[...] = jnp.zeros_like(o_ref)


def _grouped_schedule(block_start, block_count, used_blocks, n_blocks, n_tiles):
    steps = n_blocks * n_tiles
    step_end = (block_start + block_count) * n_tiles
    s = jnp.arange(steps, dtype=jnp.int32)
    e = jnp.minimum(jnp.searchsorted(step_end, s, side="right"),
                    N_EXPERTS - 1).astype(jnp.int32)
    local = s - block_start[e] * n_tiles
    nb = jnp.maximum(block_count[e], 1)
    tile = (local // nb).astype(jnp.int32)
    row = local % nb
    blk = (block_start[e] + row).astype(jnp.int32)
    live = (blk < used_blocks).astype(jnp.int32)
    first = (row == 0).astype(jnp.int32)
    nxt = s + nb
    nxt = jnp.where(nxt < steps, nxt, -1).astype(jnp.int32)
    return blk, tile, e, live, first, nxt


def _moe_glu(sched, xg, wg, wu, tn=512):
    cap, half = xg.shape
    _, d, n = wg.shape
    assert d == 2 * half
    steps = sched[0].shape[0]
    return pl.pallas_call(
        _moe_glu_kernel,
        out_shape=jax.ShapeDtypeStruct((cap, n), BF16),
        grid_spec=pltpu.PrefetchScalarGridSpec(
            num_scalar_prefetch=len(sched),
            grid=(steps,),
            in_specs=[
                pl.BlockSpec((MOE_BLOCK, half), lambda s, b, *_: (b[s], 0)),
                pl.BlockSpec(memory_space=pl.ANY),
                pl.BlockSpec(memory_space=pl.ANY),
            ],
            out_specs=pl.BlockSpec((MOE_BLOCK, tn), lambda s, b, t, *_: (b[s], t[s])),
            scratch_shapes=[pltpu.VMEM((d, tn), F32), pltpu.VMEM((d, tn), F32),
                            pltpu.VMEM((d, tn), BF16), pltpu.VMEM((d, tn), BF16),
                            pltpu.SemaphoreType.DMA((2,))]),
        compiler_params=_params(1),
        name="moe_glu",
    )(*sched, xg, wg, wu)


def _moe_down(sched, a, wd, tn=1024):
    cap, k = a.shape
    n = wd.shape[2]
    steps = sched[0].shape[0]
    return pl.pallas_call(
        _moe_down_kernel,
        out_shape=jax.ShapeDtypeStruct((cap, n), F32),
        grid_spec=pltpu.PrefetchScalarGridSpec(
            num_scalar_prefetch=len(sched),
            grid=(steps,),
            in_specs=[
                pl.BlockSpec((MOE_BLOCK, k), lambda s, b, *_: (b[s], 0)),
                pl.BlockSpec(memory_space=pl.ANY),
            ],
            out_specs=pl.BlockSpec((MOE_BLOCK, tn), lambda s, b, t, *_: (b[s], t[s])),
            scratch_shapes=[pltpu.VMEM((k, tn), F32), pltpu.VMEM((k, tn), BF16),
                            pltpu.SemaphoreType.DMA((1,))]),
        compiler_params=_params(1),
        name="moe_down",
    )(*sched, a, wd)


def _combine_kernel(pos_ref, x_ref, info_ref, g_ref, fg_ref, y_hbm, o_ref, buf, sem,
                    *, rows, n_tok):
    i = pl.program_id(0)
    n = pl.num_programs(0)

    def issue(blk, slot):
        def body(r, carry):
            for k in range(2):
                row = pos_ref[k * n_tok + blk * rows + r]
                _row_copy(y_hbm, row, buf.at[slot, k], r, sem.at[slot]).start()
            return carry
        lax.fori_loop(0, rows, body, 0, unroll=4)

    @pl.when(i == 0)
    def _():
        issue(0, 0)

    @pl.when(i + 1 < n)
    def _():
        issue(i + 1, (i + 1) % 2)

    slot = i % 2

    def wait_body(r, carry):
        for k in range(2):
            _row_copy(y_hbm, 0, buf.at[slot, k], r, sem.at[slot]).wait()
        return carry
    lax.fori_loop(0, rows, wait_body, 0, unroll=4)

    info = info_ref[...]
    f = info[:, 2:3] * buf[slot, 0] + info[:, 3:4] * buf[slot, 1]
    xn = x_ref[...] + g_ref[0] * f
    ms = jnp.mean(xn * xn, axis=-1, keepdims=True)
    o_ref[...] = xn * lax.rsqrt(ms + RMS_EPS) * fg_ref[...]


def _combine_final_norm(pos, x2, info, gate, final_g, y, seq, rows=256):
    t, d = x2.shape
    tpb = seq // rows
    return pl.pallas_call(
        functools.partial(_combine_kernel, rows=rows, n_tok=t),
        out_shape=jax.ShapeDtypeStruct((t, d), F32),
        grid_spec=pltpu.PrefetchScalarGridSpec(
            num_scalar_prefetch=1,
            grid=(t // rows,),
            in_specs=[
                pl.BlockSpec((rows, d), lambda i, p: (i, 0)),
                pl.BlockSpec((rows, LANES), lambda i, p: (i, 0)),
                pl.BlockSpec((1, 1, d), lambda i, p: (i // tpb, 0, 0)),
                pl.BlockSpec((1, d), lambda i, p: (0, 0)),
                pl.BlockSpec(memory_space=pl.ANY),
            ],
            out_specs=pl.BlockSpec((rows, d), lambda i, p: (i, 0)),
            scratch_shapes=[pltpu.VMEM((2, 2, rows, d), F32),
                            pltpu.SemaphoreType.DMA((2,))]),
        compiler_params=_params(1),
        name="moe_combine_final_norm",
    )(pos, x2, info, gate, final_g.reshape(1, d), y)


def _moe_layer_and_final_norm(x2, g, shift, scale, gate, router_w, wg, wu, wd,
                              final_g, seq):
    t, d = x2.shape
    rw_pad = jnp.pad(router_w, ((0, 0), (0, LANES - N_EXPERTS)))
    hp, info = _norm_route(x2, g, shift, scale, rw_pad, seq)

    e_flat = info[:, 0:2].astype(jnp.int32).reshape(-1)
    n_assign = e_flat.shape[0]
    onehot = (e_flat[:, None] == jnp.arange(N_EXPERTS, dtype=jnp.int32)).astype(jnp.int32)
    csum = jnp.cumsum(onehot, axis=0)
    rank = jnp.sum((csum - onehot) * onehot, axis=1)
    counts = csum[-1]
    padded = (counts + MOE_BLOCK - 1) // MOE_BLOCK * MOE_BLOCK
    pend = jnp.cumsum(padded)
    pstart = pend - padded
    dest = (pstart[e_flat] + rank).astype(jnp.int32)
    n_blocks = -(-n_assign // MOE_BLOCK) + N_EXPERTS
    cap = n_blocks * MOE_BLOCK
    tok_flat = jnp.arange(n_assign, dtype=jnp.int32) // 2
    tok_buf = jnp.full((cap,), t - 1, jnp.int32).at[dest].set(tok_flat)

    block_start = (pstart // MOE_BLOCK).astype(jnp.int32)
    block_count = (padded // MOE_BLOCK).astype(jnp.int32)
    used_blocks = pend[-1] // MOE_BLOCK
    block_count = block_count.at[N_EXPERTS - 1].set(n_blocks - block_start[N_EXPERTS - 1])

    xg = _dispatch(tok_buf, hp)

    def sched(n_tiles):
        return _grouped_schedule(block_start, block_count, used_blocks, n_blocks, n_tiles)

    tn_glu, tn_down = 512, 1024
    hmid = _moe_glu(sched(wg.shape[2] // tn_glu), xg, wg, wu, tn=tn_glu)
    y = _moe_down(sched(wd.shape[2] // tn_down), hmid, wd, tn=tn_down)

    pos = dest.reshape(t, 2).T.reshape(-1)
    return _combine_final_norm(pos, x2, info, gate, final_g, y, seq)


def kernel(x, c, norm_mix_g, norm_ffn_g, w_ada, b_ada, w_in, w_out, sc_conv_w, cf_conv_w,
           cf_conv_b, cf_ln_g, cf_ln_b, ffn_w_gate, ffn_w_up, ffn_w_down, router_w,
           moe_w_gate, moe_w_up, moe_w_down, final_g):
    bsz, seq, d = x.shape
    depth = w_ada.shape[0]
    assert depth == 2 and d == D_MODEL
    t = bsz * seq
    x2 = x.reshape(t, d)

    c_pad = jnp.pad(c, ((0, MOD_ROWS - bsz), (0, 0)))
    mod = _adaln_mod(c_pad, w_ada, b_ada)
    mod = mod[:, :bsz].reshape(depth, bsz, N_MOD, 1, d)

    w_in_b = w_in.astype(BF16)
    w_out_b = w_out.astype(BF16)
    ffn_wg_b = ffn_w_gate.astype(BF16)
    ffn_wu_b = ffn_w_up.astype(BF16)
    ffn_wd_b = ffn_w_down.astype(BF16)

    out = None
    for l in range(depth):
        sh1, sc1, g1, sh2, sc2, g2 = (mod[l, :, m] for m in range(N_MOD))

        h = _norm_modulate(x2, norm_mix_g[l], sh1, sc1, seq)
        z = _matmul(h, w_in_b, l, F32)
        y_sc, y_cf = _mixers(z, sc_conv_w[l], cf_conv_w[l], cf_conv_b[l],
                             cf_ln_g[l], cf_ln_b[l], seq)
        x2 = _out_proj_residual(y_sc, y_cf, w_out_b, l, x2, g1, seq)

        i = l // 2
        if l % 2 == 0:
            h = _norm_modulate(x2, norm_ffn_g[l], sh2, sc2, seq)
            hmid = _glu(h, ffn_wg_b, ffn_wu_b, i)
            x2 = _down_residual(hmid, ffn_wd_b, i, x2, g2, seq)
        else:
            out = _moe_layer_and_final_norm(
                x2, norm_ffn_g[l], sh2, sc2, g2, router_w[i],
                moe_w_gate[i], moe_w_up[i], moe_w_down[i], final_g, seq)
    return out.reshape(bsz, seq, d)
```

```python
import functools

import jax
import jax.numpy as jnp
from jax import lax
from jax.experimental import pallas as pl
from jax.experimental.pallas import tpu as pltpu

F32 = jnp.float32
BF16 = jnp.bfloat16

D_MODEL = 4096
D_SC = 2048
D_CF = 2048
D_IN = 3 * D_SC + 2 * D_CF
SC_WIDTH = 3
CF_WIDTH = 31
N_EXPERTS = 8
D_FF_EXPERT = 4096
MOE_BLOCK = 512
N_MOD = 6
RMS_EPS = 1e-6
LN_EPS = 1e-5

V7X_VMEM_LIMIT = 56 * 1024 * 1024
LANES = 128
SUBLANES = 8
MOD_ROWS = 8


def _params(n_axes, vmem=V7X_VMEM_LIMIT, flags=None):
    return pltpu.CompilerParams(
        dimension_semantics=("arbitrary",) * n_axes, vmem_limit_bytes=vmem, flags=flags)


def _silu(v):
    return v * jax.nn.sigmoid(v)


def _mod_kernel(c_ref, w_ref, b_ref, o_ref):
    c = c_ref[...]
    acc = jnp.dot(_silu(c).astype(BF16), w_ref[0].astype(BF16),
                  preferred_element_type=F32)
    o_ref[0] = acc + b_ref[0]


def _adaln_mod(c_pad, w_ada, b_ada, tn=512):
    depth, d, n = w_ada.shape
    return pl.pallas_call(
        _mod_kernel,
        out_shape=jax.ShapeDtypeStruct((depth, MOD_ROWS, n), F32),
        grid=(depth, n // tn),
        in_specs=[
            pl.BlockSpec((MOD_ROWS, d), lambda l, j: (0, 0)),
            pl.BlockSpec((1, d, tn), lambda l, j: (l, 0, j)),
            pl.BlockSpec((1, 1, tn), lambda l, j: (l, 0, j)),
        ],
        out_specs=pl.BlockSpec((1, MOD_ROWS, tn), lambda l, j: (l, 0, j)),
        compiler_params=_params(2),
        name="adaln_mod",
    )(c_pad, w_ada, b_ada.reshape(depth, 1, n))


def _norm_mod(x, g_ref, sh_ref, sc_ref):
    ms = jnp.mean(x * x, axis=-1, keepdims=True)
    y = x * lax.rsqrt(ms + RMS_EPS) * g_ref[...]
    return y * (1.0 + sc_ref[0]) + sh_ref[0]


def _norm_mod_kernel(x_ref, g_ref, sh_ref, sc_ref, o_ref):
    o_ref[...] = _norm_mod(x_ref[...], g_ref, sh_ref, sc_ref).astype(o_ref.dtype)


def _pack_bf16_pair(lo, hi):
    def rounded_bits(v):
        b = pltpu.bitcast(v, jnp.uint32)
        return b + jnp.uint32(0x7FFF) + ((b >> 16) & jnp.uint32(1))
    return (rounded_bits(hi) & jnp.uint32(0xFFFF0000)) | (rounded_bits(lo) >> 16)


def _unpack_bf16_pair(w):
    lo = pltpu.bitcast(w << 16, F32).astype(BF16)
    hi = pltpu.bitcast(w & jnp.uint32(0xFFFF0000), F32).astype(BF16)
    return lo, hi


def _norm_route_kernel(x_ref, g_ref, sh_ref, sc_ref, rw_ref, hp_ref, info_ref):
    h = _norm_mod(x_ref[...], g_ref, sh_ref, sc_ref)
    half = h.shape[1] // 2
    hp_ref[...] = _pack_bf16_pair(h[:, :half], h[:, half:])
    rw = rw_ref[...]
    h_hi = h.astype(BF16)
    h_lo = (h - h_hi.astype(F32)).astype(BF16)
    rw_hi = rw.astype(BF16)
    rw_lo = (rw - rw_hi.astype(F32)).astype(BF16)
    logits = (jnp.dot(h_hi, rw_hi, preferred_element_type=F32)
              + jnp.dot(h_lo, rw_hi, preferred_element_type=F32)
              + jnp.dot(h_hi, rw_lo, preferred_element_type=F32))
    lane = lax.broadcasted_iota(jnp.int32, logits.shape, 1)
    neg = jnp.float32(-jnp.inf)
    lg = jnp.where(lane < N_EXPERTS, logits, neg)
    m1 = jnp.max(lg, axis=-1, keepdims=True)
    i1 = jnp.min(jnp.where(lg == m1, lane, LANES), axis=-1, keepdims=True)
    lg2 = jnp.where(lane == i1, neg, lg)
    m2 = jnp.max(lg2, axis=-1, keepdims=True)
    i2 = jnp.min(jnp.where(lg2 == m2, lane, LANES), axis=-1, keepdims=True)
    e2 = jnp.exp(m2 - m1)
    den = 1.0 + e2
    info = jnp.where(lane == 0, i1.astype(F32),
           jnp.where(lane == 1, i2.astype(F32),
           jnp.where(lane == 2, 1.0 / den,
           jnp.where(lane == 3, e2 / den, 0.0))))
    info_ref[...] = info


def _row_specs(tm, d, tiles_per_batch):
    return [
        pl.BlockSpec((tm, d), lambda i: (i, 0)),
        pl.BlockSpec((1, d), lambda i: (0, 0)),
        pl.BlockSpec((1, 1, d), lambda i: (i // tiles_per_batch, 0, 0)),
        pl.BlockSpec((1, 1, d), lambda i: (i // tiles_per_batch, 0, 0)),
    ]


def _norm_modulate(x2, g, shift, scale, seq, tm=512):
    t, d = x2.shape
    return pl.pallas_call(
        _norm_mod_kernel,
        out_shape=jax.ShapeDtypeStruct((t, d), BF16),
        grid=(t // tm,),
        in_specs=_row_specs(tm, d, seq // tm),
        out_specs=pl.BlockSpec((tm, d), lambda i: (i, 0)),
        compiler_params=_params(1),
        name="norm_modulate",
    )(x2, g.reshape(1, d), shift, scale)


def _norm_route(x2, g, shift, scale, rw_pad, seq, tm=256):
    t, d = x2.shape
    return pl.pallas_call(
        _norm_route_kernel,
        out_shape=(jax.ShapeDtypeStruct((t, d // 2), jnp.uint32),
                   jax.ShapeDtypeStruct((t, LANES), F32)),
        grid=(t // tm,),
        in_specs=_row_specs(tm, d, seq // tm) + [
            pl.BlockSpec((d, LANES), lambda i: (0, 0))],
        out_specs=(pl.BlockSpec((tm, d // 2), lambda i: (i, 0)),
                   pl.BlockSpec((tm, LANES), lambda i: (i, 0))),
        compiler_params=_params(1),
        name="norm_route",
    )(x2, g.reshape(1, d), shift, scale, rw_pad)


def _out_proj_kernel(a1_ref, a2_ref, w1_ref, w2_ref, x_ref, g_ref, o_ref):
    acc = jnp.dot(a1_ref[...], w1_ref[0], preferred_element_type=F32)
    acc = acc + jnp.dot(a2_ref[...], w2_ref[0], preferred_element_type=F32)
    o_ref[...] = x_ref[...] + g_ref[0] * acc


def _out_proj_residual(y_sc, y_cf, w_out, layer, x2, gate, seq, tm=1024, tn=512):
    m, k1 = y_sc.shape
    k2 = y_cf.shape[1]
    n = w_out.shape[2]
    assert k1 == k2
    tpb = seq // tm
    return pl.pallas_call(
        _out_proj_kernel,
        out_shape=jax.ShapeDtypeStruct((m, n), F32),
        grid=(m // tm, n // tn),
        in_specs=[pl.BlockSpec((tm, k1), lambda i, j: (i, 0)),
                  pl.BlockSpec((tm, k2), lambda i, j: (i, 0)),
                  pl.BlockSpec((1, k1, tn), lambda i, j: (layer, 0, j)),
                  pl.BlockSpec((1, k2, tn), lambda i, j: (layer, 1, j)),
                  pl.BlockSpec((tm, tn), lambda i, j: (i, j)),
                  pl.BlockSpec((1, 1, tn), lambda i, j: (i // tpb, 0, j))],
        out_specs=pl.BlockSpec((tm, tn), lambda i, j: (i, j)),
        compiler_params=_params(2),
        name="out_proj_residual",
    )(y_sc, y_cf, w_out, w_out, x2, gate)


def _glu_kernel(a_ref, wg_ref, wu_ref, o_ref):
    a = a_ref[...]
    p = jnp.dot(a, wg_ref[0], preferred_element_type=F32)
    q = jnp.dot(a, wu_ref[0], preferred_element_type=F32)
    o_ref[...] = (_silu(p) * q).astype(o_ref.dtype)


def _glu(a, wg, wu, layer, tm=1024, tn=256):
    m, k = a.shape
    n = wg.shape[2]
    assert n % tn == 0
    return pl.pallas_call(
        _glu_kernel,
        out_shape=jax.ShapeDtypeStruct((m, n), BF16),
        grid=(m // tm, n // tn),
        in_specs=[pl.BlockSpec((tm, k), lambda i, j: (i, 0)),
                  pl.BlockSpec((1, k, tn), lambda i, j: (layer, 0, j)),
                  pl.BlockSpec((1, k, tn), lambda i, j: (layer, 0, j))],
        out_specs=pl.BlockSpec((tm, tn), lambda i, j: (i, j)),
        compiler_params=_params(2),
        name="glu",
    )(a, wg, wu)


def _down_residual_kernel(a_ref, w_ref, x_ref, g_ref, o_ref):
    acc = jnp.dot(a_ref[...], w_ref[0], preferred_element_type=F32)
    o_ref[...] = x_ref[...] + g_ref[0] * acc


def _down_residual(a, w, layer, x2, gate, seq, tm=512, tn=256):
    m, k = a.shape
    n = w.shape[2]
    tpb = seq // tm
    return pl.pallas_call(
        _down_residual_kernel,
        out_shape=jax.ShapeDtypeStruct((m, n), F32),
        grid=(m // tm, n // tn),
        in_specs=[pl.BlockSpec((tm, k), lambda i, j: (i, 0)),
                  pl.BlockSpec((1, k, tn), lambda i, j: (layer, 0, j)),
                  pl.BlockSpec((tm, tn), lambda i, j: (i, j)),
                  pl.BlockSpec((1, 1, tn), lambda i, j: (i // tpb, 0, j))],
        out_specs=pl.BlockSpec((tm, tn), lambda i, j: (i, j)),
        compiler_params=_params(2),
        name="down_residual",
    )(a, w, x2, gate)


SC_HALO = SUBLANES
CF_HALO = 32
MIX_TC = 256


N_PROJ = 5


def _proj_mixer_kernel(h_ref, w_hbm, scw_ref, cfw_ref, cfb_ref, ysc_ref, conv_ref,
                       stage_s, w_s, cv_s, u_s, sh_s, sem, *, layer, tm, tiles_per_seq):
    j = pl.program_id(0)
    i = pl.program_id(1)
    n_chunks = pl.num_programs(0)

    def weight_copies(chunk):
        return [pltpu.make_async_copy(
            w_hbm.at[layer, :, pl.ds(pl.multiple_of((n * n_chunks + chunk) * MIX_TC, MIX_TC),
                                     MIX_TC)],
            stage_s.at[n], sem.at[n]) for n in range(N_PROJ)]

    @pl.when((j == 0) & (i == 0))
    def _():
        for c in weight_copies(0):
            c.start()

    @pl.when(i == 0)
    def _():
        for c in weight_copies(j):
            c.wait()
        for n in range(N_PROJ):
            w_s[n] = stage_s[n].astype(BF16)

        @pl.when(j + 1 < n_chunks)
        def _():
            for c in weight_copies(j + 1):
                c.start()

    seq_start = (i % tiles_per_seq) == 0
    h = h_ref[...]

    def proj(n):
        return jnp.dot(h, w_s[n], preferred_element_type=F32)

    @pl.when(seq_start)
    def _():
        u_s[0:CF_HALO, :] = jnp.zeros((CF_HALO, MIX_TC), F32)
        cv_s[0:SC_HALO, :] = jnp.zeros((SC_HALO, MIX_TC), F32)

    u_s[CF_HALO:, :] = proj(3) * jax.nn.sigmoid(proj(4))
    cv_s[SC_HALO:, :] = proj(1) * proj(2)
    b_gate = proj(0)
    span = tm + CF_HALO - SUBLANES
    for r in range(1, SUBLANES):
        sh_s[r - 1] = u_s[r:r + span, :]
    cfw = cfw_ref[...]
    acc = jnp.broadcast_to(cfb_ref[...], (tm, MIX_TC))
    base = CF_HALO - (CF_WIDTH - 1)
    for k in range(CF_WIDTH):
        q, r = divmod(base + k, SUBLANES)
        if r == 0:
            tap = u_s[q * SUBLANES:q * SUBLANES + tm, :]
        else:
            tap = sh_s[r - 1, q * SUBLANES:q * SUBLANES + tm, :]
        acc = acc + cfw[k:k + 1] * tap
    conv_ref[...] = acc
    u_s[0:CF_HALO, :] = u_s[tm:tm + CF_HALO, :]

    scw = scw_ref[...]
    y = scw[0:1] * cv_s[SC_HALO - 2:SC_HALO - 2 + tm, :]
    y = y + scw[1:2] * cv_s[SC_HALO - 1:SC_HALO - 1 + tm, :]
    y = y + scw[2:3] * cv_s[SC_HALO:SC_HALO + tm, :]
    ysc_ref[...] = (b_gate * y).astype(ysc_ref.dtype)
    cv_s[0:SC_HALO, :] = cv_s[tm:tm + SC_HALO, :]


def _ln_silu_kernel(u_ref, g_ref, b_ref, o_ref):
    u = u_ref[...]
    mu = jnp.mean(u, axis=-1, keepdims=True)
    uc = u - mu
    var = jnp.mean(uc * uc, axis=-1, keepdims=True)
    yn = uc * lax.rsqrt(var + LN_EPS) * g_ref[...] + b_ref[...]
    o_ref[...] = _silu(yn).astype(o_ref.dtype)


def _proj_mixers(h, w_in, layer, sc_w, cf_w, cf_b, seq, tm=512):
    t, d = h.shape
    tc = MIX_TC
    nch = D_SC // tc
    assert D_SC == D_CF and w_in.shape[2] == N_PROJ * D_SC and seq % tm == 0

    kern = functools.partial(_proj_mixer_kernel, layer=layer, tm=tm,
                             tiles_per_seq=seq // tm)
    return pl.pallas_call(
        kern,
        out_shape=(jax.ShapeDtypeStruct((t, D_SC), BF16),
                   jax.ShapeDtypeStruct((t, D_CF), F32)),
        grid=(nch, t // tm),
        in_specs=[
            pl.BlockSpec((tm, d), lambda j, i: (i, 0)),
            pl.BlockSpec(memory_space=pl.ANY),
            pl.BlockSpec((SC_WIDTH, tc), lambda j, i: (0, j)),
            pl.BlockSpec((CF_WIDTH, tc), lambda j, i: (0, j)),
            pl.BlockSpec((1, tc), lambda j, i: (0, j)),
        ],
        out_specs=(pl.BlockSpec((tm, tc), lambda j, i: (i, j)),
                   pl.BlockSpec((tm, tc), lambda j, i: (i, j))),
        scratch_shapes=[pltpu.VMEM((N_PROJ, d, tc), F32),
                        pltpu.VMEM((N_PROJ, d, tc), BF16),
                        pltpu.VMEM((tm + SC_HALO, tc), F32),
                        pltpu.VMEM((tm + CF_HALO, tc), F32),
                        pltpu.VMEM((SUBLANES - 1, tm + CF_HALO - SUBLANES, tc), F32),
                        pltpu.SemaphoreType.DMA((N_PROJ,))],
        compiler_params=_params(2, vmem=62 * 1024 * 1024),
        name="proj_mixers",
    )(h, w_in, sc_w, cf_w, cf_b.reshape(1, D_CF))


def _ln_silu(u, ln_g, ln_b, tm=512):
    t, n = u.shape
    return pl.pallas_call(
        _ln_silu_kernel,
        out_shape=jax.ShapeDtypeStruct((t, n), BF16),
        grid=(t // tm,),
        in_specs=[pl.BlockSpec((tm, n), lambda i: (i, 0)),
                  pl.BlockSpec((1, n), lambda i: (0, 0)),
                  pl.BlockSpec((1, n), lambda i: (0, 0))],
        out_specs=pl.BlockSpec((tm, n), lambda i: (i, 0)),
        compiler_params=_params(1),
        name="ln_silu",
    )(u, ln_g.reshape(1, n), ln_b.reshape(1, n))


def _row_copy(src_hbm, row, dst, r, sem):
    return pltpu.make_async_copy(src_hbm.at[pl.ds(row, 1)], dst.at[pl.ds(r, 1)], sem)


def _dispatch_kernel(tok_ref, src_hbm, o_ref, sem, *, rows):
    base = pl.program_id(0) * rows

    def issue(r, carry):
        _row_copy(src_hbm, tok_ref[base + r], o_ref, r, sem).start()
        return carry
    lax.fori_loop(0, rows, issue, 0, unroll=8)

    def drain(r, carry):
        _row_copy(src_hbm, 0, o_ref, r, sem).wait()
        return carry
    lax.fori_loop(0, rows, drain, 0, unroll=8)


def _dispatch(tok_buf, hp, rows=MOE_BLOCK):
    cap = tok_buf.shape[0]
    width = hp.shape[1]
    return pl.pallas_call(
        functools.partial(_dispatch_kernel, rows=rows),
        out_shape=jax.ShapeDtypeStruct((cap, width), hp.dtype),
        grid_spec=pltpu.PrefetchScalarGridSpec(
            num_scalar_prefetch=1,
            grid=(cap // rows,),
            in_specs=[pl.BlockSpec(memory_space=pl.ANY)],
            out_specs=pl.BlockSpec((rows, width), lambda i, tok: (i, 0)),
            scratch_shapes=[pltpu.SemaphoreType.DMA]),
        compiler_params=_params(1),
        name="moe_dispatch",
    )(tok_buf, hp)


def _weight_tile_copy(w_hbm, e, t, stage, sem):
    tn = stage.shape[1]
    col = pl.multiple_of(t * tn, tn)
    return pltpu.make_async_copy(w_hbm.at[e, :, pl.ds(col, tn)], stage, sem)


def _refresh_weights(s, tile_ref, exp_ref, first_ref, next_ref, w_hbms, stages, w_bf16s, sem):
    def copies(step):
        return [_weight_tile_copy(w, exp_ref[step], tile_ref[step], st, sem.at[n])
                for n, (w, st) in enumerate(zip(w_hbms, stages))]

    @pl.when(s == 0)
    def _():
        for c in copies(0):
            c.start()

    @pl.when(first_ref[s] == 1)
    def _():
        for c in copies(s):
            c.wait()
        for st, wb in zip(stages, w_bf16s):
            wb[...] = st[...].astype(BF16)
        nxt = next_ref[s]

        @pl.when(nxt >= 0)
        def _():
            for c in copies(nxt):
                c.start()


def _moe_glu_kernel(blk_ref, tile_ref, exp_ref, live_ref, first_ref, next_ref,
                    x_ref, wg_hbm, wu_hbm, o_ref, stage_g, stage_u, wg_s, wu_s, sem):
    s = pl.program_id(0)
    _refresh_weights(s, tile_ref, exp_ref, first_ref, next_ref,
                     (wg_hbm, wu_hbm), (stage_g, stage_u), (wg_s, wu_s), sem)

    @pl.when(live_ref[s] == 1)
    def _():
        x = jnp.concatenate(_unpack_bf16_pair(x_ref[...]), axis=-1)
        p = jnp.dot(x, wg_s[...], preferred_element_type=F32)
        q = jnp.dot(x, wu_s[...], preferred_element_type=F32)
        o_ref[...] = (_silu(p) * q).astype(o_ref.dtype)

    @pl.when(live_ref[s] == 0)
    def _():
        o_ref[...] = jnp.zeros_like(o_ref)


def _moe_down_kernel(blk_ref, tile_ref, exp_ref, live_ref, first_ref, next_ref,
                     a_ref, w_hbm, o_ref, stage, w_s, sem):
    s = pl.program_id(0)
    _refresh_weights(s, tile_ref, exp_ref, first_ref, next_ref,
                     (w_hbm,), (stage,), (w_s,), sem)

    @pl.when(live_ref[s] == 1)
    def _():
        o_ref[...] = jnp.dot(a_ref[...], w_s[...], preferred_element_type=F32)

    @pl.when(live_ref[s] == 0)
    def _():
        o_ref[...] = jnp.zeros_like(o_ref)


def _grouped_schedule(block_start, block_count, used_blocks, n_blocks, n_tiles):
    steps = n_blocks * n_tiles
    step_end = (block_start + block_count) * n_tiles
    s = jnp.arange(steps, dtype=jnp.int32)
    e = jnp.minimum(jnp.searchsorted(step_end, s, side="right"),
                    N_EXPERTS - 1).astype(jnp.int32)
    local = s - block_start[e] * n_tiles
    nb = jnp.maximum(block_count[e], 1)
    tile = (local // nb).astype(jnp.int32)
    row = local % nb
    blk = (block_start[e] + row).astype(jnp.int32)
    live = (blk < used_blocks).astype(jnp.int32)
    first = (row == 0).astype(jnp.int32)
    nxt = s + nb
    nxt = jnp.where(nxt < steps, nxt, -1).astype(jnp.int32)
    return blk, tile, e, live, first, nxt


def _moe_glu(sched, xg, wg, wu, tn=512):
    cap, half = xg.shape
    _, d, n = wg.shape
    assert d == 2 * half
    steps = sched[0].shape[0]
    return pl.pallas_call(
        _moe_glu_kernel,
        out_shape=jax.ShapeDtypeStruct((cap, n), BF16),
        grid_spec=pltpu.PrefetchScalarGridSpec(
            num_scalar_prefetch=len(sched),
            grid=(steps,),
            in_specs=[
                pl.BlockSpec((MOE_BLOCK, half), lambda s, b, *_: (b[s], 0)),
                pl.BlockSpec(memory_space=pl.ANY),
                pl.BlockSpec(memory_space=pl.ANY),
            ],
            out_specs=pl.BlockSpec((MOE_BLOCK, tn), lambda s, b, t, *_: (b[s], t[s])),
            scratch_shapes=[pltpu.VMEM((d, tn), F32), pltpu.VMEM((d, tn), F32),
                            pltpu.VMEM((d, tn), BF16), pltpu.VMEM((d, tn), BF16),
                            pltpu.SemaphoreType.DMA((2,))]),
        compiler_params=_params(1),
        name="moe_glu",
    )(*sched, xg, wg, wu)


def _moe_down(sched, a, wd, tn=1024):
    cap, k = a.shape
    n = wd.shape[2]
    steps = sched[0].shape[0]
    return pl.pallas_call(
        _moe_down_kernel,
        out_shape=jax.ShapeDtypeStruct((cap, n), F32),
        grid_spec=pltpu.PrefetchScalarGridSpec(
            num_scalar_prefetch=len(sched),
            grid=(steps,),
            in_specs=[
                pl.BlockSpec((MOE_BLOCK, k), lambda s, b, *_: (b[s], 0)),
                pl.BlockSpec(memory_space=pl.ANY),
            ],
            out_specs=pl.BlockSpec((MOE_BLOCK, tn), lambda s, b, t, *_: (b[s], t[s])),
            scratch_shapes=[pltpu.VMEM((k, tn), F32), pltpu.VMEM((k, tn), BF16),
                            pltpu.SemaphoreType.DMA((1,))]),
        compiler_params=_params(1),
        name="moe_down",
    )(*sched, a, wd)


def _combine_kernel(pos_ref, x_ref, info_ref, g_ref, fg_ref, y_hbm, o_ref, buf, sem,
                    *, rows, n_tok):
    i = pl.program_id(0)
    n = pl.num_programs(0)

    def issue(blk, slot):
        def body(r, carry):
            for k in range(2):
                row = pos_ref[k * n_tok + blk * rows + r]
                _row_copy(y_hbm, row, buf.at[slot, k], r, sem.at[slot]).start()
            return carry
        lax.fori_loop(0, rows, body, 0, unroll=4)

    @pl.when(i == 0)
    def _():
        issue(0, 0)

    @pl.when(i + 1 < n)
    def _():
        issue(i + 1, (i + 1) % 2)

    slot = i % 2

    def wait_body(r, carry):
        for k in range(2):
            _row_copy(y_hbm, 0, buf.at[slot, k], r, sem.at[slot]).wait()
        return carry
    lax.fori_loop(0, rows, wait_body, 0, unroll=4)

    info = info_ref[...]
    f = info[:, 2:3] * buf[slot, 0] + info[:, 3:4] * buf[slot, 1]
    xn = x_ref[...] + g_ref[0] * f
    ms = jnp.mean(xn * xn, axis=-1, keepdims=True)
    o_ref[...] = xn * lax.rsqrt(ms + RMS_EPS) * fg_ref[...]


def _combine_final_norm(pos, x2, info, gate, final_g, y, seq, rows=256):
    t, d = x2.shape
    tpb = seq // rows
    return pl.pallas_call(
        functools.partial(_combine_kernel, rows=rows, n_tok=t),
        out_shape=jax.ShapeDtypeStruct((t, d), F32),
        grid_spec=pltpu.PrefetchScalarGridSpec(
            num_scalar_prefetch=1,
            grid=(t // rows,),
            in_specs=[
                pl.BlockSpec((rows, d), lambda i, p: (i, 0)),
                pl.BlockSpec((rows, LANES), lambda i, p: (i, 0)),
                pl.BlockSpec((1, 1, d), lambda i, p: (i // tpb, 0, 0)),
                pl.BlockSpec((1, d), lambda i, p: (0, 0)),
                pl.BlockSpec(memory_space=pl.ANY),
            ],
            out_specs=pl.BlockSpec((rows, d), lambda i, p: (i, 0)),
            scratch_shapes=[pltpu.VMEM((2, 2, rows, d), F32),
                            pltpu.SemaphoreType.DMA((2,))]),
        compiler_params=_params(1),
        name="moe_combine_final_norm",
    )(pos, x2, info, gate, final_g.reshape(1, d), y)


def _moe_layer_and_final_norm(x2, g, shift, scale, gate, router_w, wg, wu, wd,
                              final_g, seq):
    t, d = x2.shape
    rw_pad = jnp.pad(router_w, ((0, 0), (0, LANES - N_EXPERTS)))
    hp, info = _norm_route(x2, g, shift, scale, rw_pad, seq)

    e_flat = info[:, 0:2].astype(jnp.int32).reshape(-1)
    n_assign = e_flat.shape[0]
    onehot = (e_flat[:, None] == jnp.arange(N_EXPERTS, dtype=jnp.int32)).astype(jnp.int32)
    csum = jnp.cumsum(onehot, axis=0)
    rank = jnp.sum((csum - onehot) * onehot, axis=1)
    counts = csum[-1]
    padded = (counts + MOE_BLOCK - 1) // MOE_BLOCK * MOE_BLOCK
    pend = jnp.cumsum(padded)
    pstart = pend - padded
    dest = (pstart[e_flat] + rank).astype(jnp.int32)
    n_blocks = -(-n_assign // MOE_BLOCK) + N_EXPERTS
    cap = n_blocks * MOE_BLOCK
    tok_flat = jnp.arange(n_assign, dtype=jnp.int32) // 2
    tok_buf = jnp.full((cap,), t - 1, jnp.int32).at[dest].set(tok_flat)

    block_start = (pstart // MOE_BLOCK).astype(jnp.int32)
    block_count = (padded // MOE_BLOCK).astype(jnp.int32)
    used_blocks = pend[-1] // MOE_BLOCK
    block_count = block_count.at[N_EXPERTS - 1].set(n_blocks - block_start[N_EXPERTS - 1])

    xg = _dispatch(tok_buf, hp)

    def sched(n_tiles):
        return _grouped_schedule(block_start, block_count, used_blocks, n_blocks, n_tiles)

    tn_glu, tn_down = 512, 1024
    hmid = _moe_glu(sched(wg.shape[2] // tn_glu), xg, wg, wu, tn=tn_glu)
    y = _moe_down(sched(wd.shape[2] // tn_down), hmid, wd, tn=tn_down)

    pos = dest.reshape(t, 2).T.reshape(-1)
    return _combine_final_norm(pos, x2, info, gate, final_g, y, seq)


def kernel(x, c, norm_mix_g, norm_ffn_g, w_ada, b_ada, w_in, w_out, sc_conv_w, cf_conv_w,
           cf_conv_b, cf_ln_g, cf_ln_b, ffn_w_gate, ffn_w_up, ffn_w_down, router_w,
           moe_w_gate, moe_w_up, moe_w_down, final_g):
    bsz, seq, d = x.shape
    depth = w_ada.shape[0]
    assert depth == 2 and d == D_MODEL
    t = bsz * seq
    x2 = x.reshape(t, d)

    c_pad = jnp.pad(c, ((0, MOD_ROWS - bsz), (0, 0)))
    mod = _adaln_mod(c_pad, w_ada, b_ada)
    mod = mod[:, :bsz].reshape(depth, bsz, N_MOD, 1, d)

    w_out_b = w_out.astype(BF16)
    ffn_wg_b = ffn_w_gate.astype(BF16)
    ffn_wu_b = ffn_w_up.astype(BF16)
    ffn_wd_b = ffn_w_down.astype(BF16)

    out = None
    for l in range(depth):
        sh1, sc1, g1, sh2, sc2, g2 = (mod[l, :, m] for m in range(N_MOD))

        h = _norm_modulate(x2, norm_mix_g[l], sh1, sc1, seq)
        y_sc, conv = _proj_mixers(h, w_in, l, sc_conv_w[l], cf_conv_w[l],
                                  cf_conv_b[l], seq)
        y_cf = _ln_silu(conv, cf_ln_g[l], cf_ln_b[l])
        x2 = _out_proj_residual(y_sc, y_cf, w_out_b, l, x2, g1, seq)

        i = l // 2
        if l % 2 == 0:
            h = _norm_modulate(x2, norm_ffn_g[l], sh2, sc2, seq)
            hmid = _glu(h, ffn_wg_b, ffn_wu_b, i)
            x2 = _down_residual(hmid, ffn_wd_b, i, x2, g2, seq)
        else:
            out = _moe_layer_and_final_norm(
                x2, norm_ffn_g[l], sh2, sc2, g2, router_w[i],
                moe_w_gate[i], moe_w_up[i], moe_w_down[i], final_g, seq)
    return out.reshape(bsz, seq, d)
```

```python
import functools

import jax
import jax.numpy as jnp
from jax import lax
from jax.experimental import pallas as pl
from jax.experimental.pallas import tpu as pltpu

F32 = jnp.float32
BF16 = jnp.bfloat16

D_MODEL = 4096
D_SC = 2048
D_CF = 2048
D_IN = 3 * D_SC + 2 * D_CF
SC_WIDTH = 3
CF_WIDTH = 31
N_EXPERTS = 8
D_FF_EXPERT = 4096
MOE_BLOCK = 512
N_MOD = 6
RMS_EPS = 1e-6
LN_EPS = 1e-5

V7X_VMEM_LIMIT = 56 * 1024 * 1024
LANES = 128
SUBLANES = 8
MOD_ROWS = 8


def _params(n_axes, vmem=V7X_VMEM_LIMIT, flags=None):
    return pltpu.CompilerParams(
        dimension_semantics=("arbitrary",) * n_axes, vmem_limit_bytes=vmem, flags=flags)


def _silu(v):
    return v * jax.nn.sigmoid(v)


def _mod_kernel(c_ref, w_ref, b_ref, o_ref):
    c = c_ref[...]
    acc = jnp.dot(_silu(c).astype(BF16), w_ref[0].astype(BF16),
                  preferred_element_type=F32)
    o_ref[0] = acc + b_ref[0]


def _adaln_mod(c_pad, w_ada, b_ada, tn=512):
    depth, d, n = w_ada.shape
    return pl.pallas_call(
        _mod_kernel,
        out_shape=jax.ShapeDtypeStruct((depth, MOD_ROWS, n), F32),
        grid=(depth, n // tn),
        in_specs=[
            pl.BlockSpec((MOD_ROWS, d), lambda l, j: (0, 0)),
            pl.BlockSpec((1, d, tn), lambda l, j: (l, 0, j)),
            pl.BlockSpec((1, 1, tn), lambda l, j: (l, 0, j)),
        ],
        out_specs=pl.BlockSpec((1, MOD_ROWS, tn), lambda l, j: (l, 0, j)),
        compiler_params=_params(2),
        name="adaln_mod",
    )(c_pad, w_ada, b_ada.reshape(depth, 1, n))


def _norm_mod(x, g_ref, sh_ref, sc_ref):
    ms = jnp.mean(x * x, axis=-1, keepdims=True)
    y = x * lax.rsqrt(ms + RMS_EPS) * g_ref[...]
    return y * (1.0 + sc_ref[0]) + sh_ref[0]


def _norm_mod_kernel(x_ref, g_ref, sh_ref, sc_ref, o_ref):
    o_ref[...] = _norm_mod(x_ref[...], g_ref, sh_ref, sc_ref).astype(o_ref.dtype)


def _pack_bf16_pair(lo, hi):
    def rounded_bits(v):
        b = pltpu.bitcast(v, jnp.uint32)
        return b + jnp.uint32(0x7FFF) + ((b >> 16) & jnp.uint32(1))
    return (rounded_bits(hi) & jnp.uint32(0xFFFF0000)) | (rounded_bits(lo) >> 16)


def _unpack_bf16_pair(w):
    lo = pltpu.bitcast(w << 16, F32).astype(BF16)
    hi = pltpu.bitcast(w & jnp.uint32(0xFFFF0000), F32).astype(BF16)
    return lo, hi


def _norm_route_kernel(x_ref, g_ref, sh_ref, sc_ref, rw_ref, hp_ref, info_ref, cnt_ref):
    @pl.when(pl.program_id(0) == 0)
    def _():
        cnt_ref[...] = jnp.zeros(cnt_ref.shape, F32)

    h = _norm_mod(x_ref[...], g_ref, sh_ref, sc_ref)
    half = h.shape[1] // 2
    hp_ref[...] = _pack_bf16_pair(h[:, :half], h[:, half:])
    rw = rw_ref[...]
    h_hi = h.astype(BF16)
    h_lo = (h - h_hi.astype(F32)).astype(BF16)
    rw_hi = rw.astype(BF16)
    rw_lo = (rw - rw_hi.astype(F32)).astype(BF16)
    logits = (jnp.dot(h_hi, rw_hi, preferred_element_type=F32)
              + jnp.dot(h_lo, rw_hi, preferred_element_type=F32)
              + jnp.dot(h_hi, rw_lo, preferred_element_type=F32))
    lane = lax.broadcasted_iota(jnp.int32, logits.shape, 1)
    neg = jnp.float32(-jnp.inf)
    lg = jnp.where(lane < N_EXPERTS, logits, neg)
    m1 = jnp.max(lg, axis=-1, keepdims=True)
    i1 = jnp.min(jnp.where(lg == m1, lane, LANES), axis=-1, keepdims=True)
    lg2 = jnp.where(lane == i1, neg, lg)
    m2 = jnp.max(lg2, axis=-1, keepdims=True)
    i2 = jnp.min(jnp.where(lg2 == m2, lane, LANES), axis=-1, keepdims=True)
    e2 = jnp.exp(m2 - m1)
    den = 1.0 + e2

    oh1 = (lane == i1).astype(F32)
    oh2 = (lane == i2).astype(F32)
    oh = oh1 + oh2
    tm = oh.shape[0]
    tri = (lax.broadcasted_iota(jnp.int32, (tm, tm), 1)
           < lax.broadcasted_iota(jnp.int32, (tm, tm), 0)).astype(BF16)
    before = jnp.dot(tri, oh.astype(BF16), preferred_element_type=F32) + cnt_ref[...]
    rank1 = jnp.sum(oh1 * before, axis=-1, keepdims=True)
    rank2 = jnp.sum(oh2 * before, axis=-1, keepdims=True)
    cnt_ref[...] = cnt_ref[...] + jnp.sum(oh, axis=0, keepdims=True)

    info = jnp.where(lane == 0, i1.astype(F32),
           jnp.where(lane == 1, i2.astype(F32),
           jnp.where(lane == 2, 1.0 / den,
           jnp.where(lane == 3, e2 / den,
           jnp.where(lane == 4, rank1,
           jnp.where(lane == 5, rank2, 0.0))))))
    info_ref[...] = info


def _row_specs(tm, d, tiles_per_batch):
    return [
        pl.BlockSpec((tm, d), lambda i: (i, 0)),
        pl.BlockSpec((1, d), lambda i: (0, 0)),
        pl.BlockSpec((1, 1, d), lambda i: (i // tiles_per_batch, 0, 0)),
        pl.BlockSpec((1, 1, d), lambda i: (i // tiles_per_batch, 0, 0)),
    ]


def _norm_modulate(x2, g, shift, scale, seq, tm=512):
    t, d = x2.shape
    return pl.pallas_call(
        _norm_mod_kernel,
        out_shape=jax.ShapeDtypeStruct((t, d), BF16),
        grid=(t // tm,),
        in_specs=_row_specs(tm, d, seq // tm),
        out_specs=pl.BlockSpec((tm, d), lambda i: (i, 0)),
        compiler_params=_params(1),
        name="norm_modulate",
    )(x2, g.reshape(1, d), shift, scale)


def _norm_route(x2, g, shift, scale, rw_pad, seq, tm=256):
    t, d = x2.shape
    return pl.pallas_call(
        _norm_route_kernel,
        out_shape=(jax.ShapeDtypeStruct((t, d // 2), jnp.uint32),
                   jax.ShapeDtypeStruct((t, LANES), F32),
                   jax.ShapeDtypeStruct((1, LANES), F32)),
        grid=(t // tm,),
        in_specs=_row_specs(tm, d, seq // tm) + [
            pl.BlockSpec((d, LANES), lambda i: (0, 0))],
        out_specs=(pl.BlockSpec((tm, d // 2), lambda i: (i, 0)),
                   pl.BlockSpec((tm, LANES), lambda i: (i, 0)),
                   pl.BlockSpec((1, LANES), lambda i: (0, 0))),
        compiler_params=_params(1),
        name="norm_route",
    )(x2, g.reshape(1, d), shift, scale, rw_pad)


def _out_proj_kernel(a1_ref, a2_ref, w_hbm, x_ref, g_ref, o_ref, stage_s, w_s, sem, *, layer):
    j = pl.program_id(0)
    i = pl.program_id(1)
    tn = stage_s.shape[1]

    def weight_copy(tile):
        col = pl.multiple_of(tile * tn, tn)
        return pltpu.make_async_copy(w_hbm.at[layer, :, pl.ds(col, tn)], stage_s, sem)

    @pl.when((j == 0) & (i == 0))
    def _():
        weight_copy(0).start()

    @pl.when(i == 0)
    def _():
        weight_copy(j).wait()
        w_s[...] = stage_s[...].astype(BF16)

        @pl.when(j + 1 < pl.num_programs(0))
        def _():
            weight_copy(j + 1).start()

    k1 = a1_ref.shape[1]
    acc = jnp.dot(a1_ref[...], w_s[:k1, :], preferred_element_type=F32)
    acc = acc + jnp.dot(a2_ref[...], w_s[k1:, :], preferred_element_type=F32)
    o_ref[...] = x_ref[...] + g_ref[0] * acc


def _out_proj_residual(y_sc, y_cf, w_out, layer, x2, gate, seq, tm=1024, tn=512):
    m, k1 = y_sc.shape
    k2 = y_cf.shape[1]
    n = w_out.shape[2]
    assert k1 + k2 == w_out.shape[1]
    tpb = seq // tm
    return pl.pallas_call(
        functools.partial(_out_proj_kernel, layer=layer),
        out_shape=jax.ShapeDtypeStruct((m, n), F32),
        grid=(n // tn, m // tm),
        in_specs=[pl.BlockSpec((tm, k1), lambda j, i: (i, 0)),
                  pl.BlockSpec((tm, k2), lambda j, i: (i, 0)),
                  pl.BlockSpec(memory_space=pl.ANY),
                  pl.BlockSpec((tm, tn), lambda j, i: (i, j)),
                  pl.BlockSpec((1, 1, tn), lambda j, i: (i // tpb, 0, j))],
        out_specs=pl.BlockSpec((tm, tn), lambda j, i: (i, j)),
        scratch_shapes=[pltpu.VMEM((k1 + k2, tn), F32),
                        pltpu.VMEM((k1 + k2, tn), BF16),
                        pltpu.SemaphoreType.DMA],
        compiler_params=_params(2),
        name="out_proj_residual",
    )(y_sc, y_cf, w_out, x2, gate)


def _glu_kernel(a_ref, wg_ref, wu_ref, o_ref):
    a = a_ref[...]
    p = jnp.dot(a, wg_ref[0], preferred_element_type=F32)
    q = jnp.dot(a, wu_ref[0], preferred_element_type=F32)
    o_ref[...] = (_silu(p) * q).astype(o_ref.dtype)


def _glu(a, wg, wu, layer, tm=1024, tn=256):
    m, k = a.shape
    n = wg.shape[2]
    assert n % tn == 0
    return pl.pallas_call(
        _glu_kernel,
        out_shape=jax.ShapeDtypeStruct((m, n), BF16),
        grid=(m // tm, n // tn),
        in_specs=[pl.BlockSpec((tm, k), lambda i, j: (i, 0)),
                  pl.BlockSpec((1, k, tn), lambda i, j: (layer, 0, j)),
                  pl.BlockSpec((1, k, tn), lambda i, j: (layer, 0, j))],
        out_specs=pl.BlockSpec((tm, tn), lambda i, j: (i, j)),
        compiler_params=_params(2),
        name="glu",
    )(a, wg, wu)


def _down_residual_kernel(a_ref, w_ref, x_ref, g_ref, o_ref):
    acc = jnp.dot(a_ref[...], w_ref[0], preferred_element_type=F32)
    o_ref[...] = x_ref[...] + g_ref[0] * acc


def _down_residual(a, w, layer, x2, gate, seq, tm=512, tn=512):
    m, k = a.shape
    n = w.shape[2]
    tpb = seq // tm
    return pl.pallas_call(
        _down_residual_kernel,
        out_shape=jax.ShapeDtypeStruct((m, n), F32),
        grid=(m // tm, n // tn),
        in_specs=[pl.BlockSpec((tm, k), lambda i, j: (i, 0)),
                  pl.BlockSpec((1, k, tn), lambda i, j: (layer, 0, j)),
                  pl.BlockSpec((tm, tn), lambda i, j: (i, j)),
                  pl.BlockSpec((1, 1, tn), lambda i, j: (i // tpb, 0, j))],
        out_specs=pl.BlockSpec((tm, tn), lambda i, j: (i, j)),
        compiler_params=_params(2),
        name="down_residual",
    )(a, w, x2, gate)


SC_HALO = SUBLANES
CF_HALO = 32
MIX_TC = 256


N_PROJ = 5


def _proj_mixer_kernel(h_ref, w_hbm, scw_ref, cfw_ref, cfb_ref, ysc_ref, conv_ref,
                       stage_s, w_s, cv_s, u_s, sh_s, sem, *, layer, tm, tiles_per_seq):
    j = pl.program_id(0)
    i = pl.program_id(1)
    n_chunks = pl.num_programs(0)

    def weight_copies(chunk):
        return [pltpu.make_async_copy(
            w_hbm.at[layer, :, pl.ds(pl.multiple_of((n * n_chunks + chunk) * MIX_TC, MIX_TC),
                                     MIX_TC)],
            stage_s.at[n], sem.at[n]) for n in range(N_PROJ)]

    @pl.when((j == 0) & (i == 0))
    def _():
        for c in weight_copies(0):
            c.start()

    @pl.when(i == 0)
    def _():
        for c in weight_copies(j):
            c.wait()
        for n in range(N_PROJ):
            w_s[n] = stage_s[n].astype(BF16)

        @pl.when(j + 1 < n_chunks)
        def _():
            for c in weight_copies(j + 1):
                c.start()

    seq_start = (i % tiles_per_seq) == 0
    h = h_ref[...]

    def proj(n):
        return jnp.dot(h, w_s[n], preferred_element_type=F32)

    @pl.when(seq_start)
    def _():
        u_s[0:CF_HALO, :] = jnp.zeros((CF_HALO, MIX_TC), F32)
        cv_s[0:SC_HALO, :] = jnp.zeros((SC_HALO, MIX_TC), F32)

    u_s[CF_HALO:, :] = proj(3) * jax.nn.sigmoid(proj(4))
    cv_s[SC_HALO:, :] = proj(1) * proj(2)
    b_gate = proj(0)
    span = tm + CF_HALO - SUBLANES
    for r in range(1, SUBLANES):
        sh_s[r - 1] = u_s[r:r + span, :]
    cfw = cfw_ref[...]
    acc = jnp.broadcast_to(cfb_ref[...], (tm, MIX_TC))
    base = CF_HALO - (CF_WIDTH - 1)
    for k in range(CF_WIDTH):
        q, r = divmod(base + k, SUBLANES)
        if r == 0:
            tap = u_s[q * SUBLANES:q * SUBLANES + tm, :]
        else:
            tap = sh_s[r - 1, q * SUBLANES:q * SUBLANES + tm, :]
        acc = acc + cfw[k:k + 1] * tap
    conv_ref[...] = acc
    u_s[0:CF_HALO, :] = u_s[tm:tm + CF_HALO, :]

    scw = scw_ref[...]
    y = scw[0:1] * cv_s[SC_HALO - 2:SC_HALO - 2 + tm, :]
    y = y + scw[1:2] * cv_s[SC_HALO - 1:SC_HALO - 1 + tm, :]
    y = y + scw[2:3] * cv_s[SC_HALO:SC_HALO + tm, :]
    ysc_ref[...] = (b_gate * y).astype(ysc_ref.dtype)
    cv_s[0:SC_HALO, :] = cv_s[tm:tm + SC_HALO, :]


def _ln_silu_kernel(u_ref, g_ref, b_ref, o_ref):
    u = u_ref[...]
    mu = jnp.mean(u, axis=-1, keepdims=True)
    uc = u - mu
    var = jnp.mean(uc * uc, axis=-1, keepdims=True)
    yn = uc * lax.rsqrt(var + LN_EPS) * g_ref[...] + b_ref[...]
    o_ref[...] = _silu(yn).astype(o_ref.dtype)


def _proj_mixers(h, w_in, layer, sc_w, cf_w, cf_b, seq, tm=512):
    t, d = h.shape
    tc = MIX_TC
    nch = D_SC // tc
    assert D_SC == D_CF and w_in.shape[2] == N_PROJ * D_SC and seq % tm == 0

    kern = functools.partial(_proj_mixer_kernel, layer=layer, tm=tm,
                             tiles_per_seq=seq // tm)
    return pl.pallas_call(
        kern,
        out_shape=(jax.ShapeDtypeStruct((t, D_SC), BF16),
                   jax.ShapeDtypeStruct((t, D_CF), F32)),
        grid=(nch, t // tm),
        in_specs=[
            pl.BlockSpec((tm, d), lambda j, i: (i, 0)),
            pl.BlockSpec(memory_space=pl.ANY),
            pl.BlockSpec((SC_WIDTH, tc), lambda j, i: (0, j)),
            pl.BlockSpec((CF_WIDTH, tc), lambda j, i: (0, j)),
            pl.BlockSpec((1, tc), lambda j, i: (0, j)),
        ],
        out_specs=(pl.BlockSpec((tm, tc), lambda j, i: (i, j)),
                   pl.BlockSpec((tm, tc), lambda j, i: (i, j))),
        scratch_shapes=[pltpu.VMEM((N_PROJ, d, tc), F32),
                        pltpu.VMEM((N_PROJ, d, tc), BF16),
                        pltpu.VMEM((tm + SC_HALO, tc), F32),
                        pltpu.VMEM((tm + CF_HALO, tc), F32),
                        pltpu.VMEM((SUBLANES - 1, tm + CF_HALO - SUBLANES, tc), F32),
                        pltpu.SemaphoreType.DMA((N_PROJ,))],
        compiler_params=_params(2, vmem=62 * 1024 * 1024),
        name="proj_mixers",
    )(h, w_in, sc_w, cf_w, cf_b.reshape(1, D_CF))


def _ln_silu(u, ln_g, ln_b, tm=512):
    t, n = u.shape
    return pl.pallas_call(
        _ln_silu_kernel,
        out_shape=jax.ShapeDtypeStruct((t, n), BF16),
        grid=(t // tm,),
        in_specs=[pl.BlockSpec((tm, n), lambda i: (i, 0)),
                  pl.BlockSpec((1, n), lambda i: (0, 0)),
                  pl.BlockSpec((1, n), lambda i: (0, 0))],
        out_specs=pl.BlockSpec((tm, n), lambda i: (i, 0)),
        compiler_params=_params(1),
        name="ln_silu",
    )(u, ln_g.reshape(1, n), ln_b.reshape(1, n))


def _row_copy(src_hbm, row, dst, r, sem):
    return pltpu.make_async_copy(src_hbm.at[pl.ds(row, 1)], dst.at[pl.ds(r, 1)], sem)


N_DMA_PRIORITIES = 2


def _dispatch_kernel(tok_ref, src_hbm, o_ref, sem, *, rows):
    base = pl.program_id(0) * rows

    def issue(g, carry):
        for p in range(N_DMA_PRIORITIES):
            r = g * N_DMA_PRIORITIES + p
            _row_copy(src_hbm, tok_ref[base + r], o_ref, r, sem).start(priority=p)
        return carry
    lax.fori_loop(0, rows // N_DMA_PRIORITIES, issue, 0, unroll=4)

    def drain(r, carry):
        _row_copy(src_hbm, 0, o_ref, r, sem).wait()
        return carry
    lax.fori_loop(0, rows, drain, 0, unroll=8)


def _dispatch(tok_buf, hp, rows=MOE_BLOCK):
    cap = tok_buf.shape[0]
    width = hp.shape[1]
    return pl.pallas_call(
        functools.partial(_dispatch_kernel, rows=rows),
        out_shape=jax.ShapeDtypeStruct((cap, width), hp.dtype),
        grid_spec=pltpu.PrefetchScalarGridSpec(
            num_scalar_prefetch=1,
            grid=(cap // rows,),
            in_specs=[pl.BlockSpec(memory_space=pl.ANY)],
            out_specs=pl.BlockSpec((rows, width), lambda i, tok: (i, 0)),
            scratch_shapes=[pltpu.SemaphoreType.DMA]),
        compiler_params=_params(1),
        name="moe_dispatch",
    )(tok_buf, hp)


def _weight_tile_copy(w_hbm, e, t, stage, sem):
    tn = stage.shape[1]
    col = pl.multiple_of(t * tn, tn)
    return pltpu.make_async_copy(w_hbm.at[e, :, pl.ds(col, tn)], stage, sem)


def _refresh_weights(s, tile_ref, exp_ref, first_ref, next_ref, w_hbms, stages, w_bf16s, sem):
    def copies(step):
        return [_weight_tile_copy(w, exp_ref[step], tile_ref[step], st, sem.at[n])
                for n, (w, st) in enumerate(zip(w_hbms, stages))]

    @pl.when(s == 0)
    def _():
        for c in copies(0):
            c.start()

    @pl.when(first_ref[s] == 1)
    def _():
        for c in copies(s):
            c.wait()
        for st, wb in zip(stages, w_bf16s):
            wb[...] = st[...].astype(BF16)
        nxt = next_ref[s]

        @pl.when(nxt >= 0)
        def _():
            for c in copies(nxt):
                c.start()


def _moe_glu_kernel(blk_ref, tile_ref, exp_ref, live_ref, first_ref, next_ref,
                    x_ref, wg_hbm, wu_hbm, o_ref, stage_g, stage_u, wg_s, wu_s, sem):
    s = pl.program_id(0)
    _refresh_weights(s, tile_ref, exp_ref, first_ref, next_ref,
                     (wg_hbm, wu_hbm), (stage_g, stage_u), (wg_s, wu_s), sem)

    @pl.when(live_ref[s] == 1)
    def _():
        x = jnp.concatenate(_unpack_bf16_pair(x_ref[...]), axis=-1)
        p = jnp.dot(x, wg_s[...], preferred_element_type=F32)
        q = jnp.dot(x, wu_s[...], preferred_element_type=F32)
        o_ref[...] = (_silu(p) * q).astype(o_ref.dtype)

    @pl.when(live_ref[s] == 0)
    def _():
        o_ref[...] = jnp.zeros_like(o_ref)


def _moe_down_kernel(blk_ref, tile_ref, exp_ref, live_ref, first_ref, next_ref,
                     a_ref, w_hbm, o_ref, stage, w_s, sem):
    s = pl.program_id(0)
    _refresh_weights(s, tile_ref, exp_ref, first_ref, next_ref,
                     (w_hbm,), (stage,), (w_s,), sem)

    @pl.when(live_ref[s] == 1)
    def _():
        o_ref[...] = jnp.dot(a_ref[...], w_s[...], preferred_element_type=F32)

    @pl.when(live_ref[s] == 0)
    def _():
        o_ref[...] = jnp.zeros_like(o_ref)


def _grouped_schedule(block_start, block_count, used_blocks, n_blocks, n_tiles):
    steps = n_blocks * n_tiles
    step_end = (block_start + block_count) * n_tiles
    s = jnp.arange(steps, dtype=jnp.int32)
    e = jnp.minimum(jnp.searchsorted(step_end, s, side="right"),
                    N_EXPERTS - 1).astype(jnp.int32)
    local = s - block_start[e] * n_tiles
    nb = jnp.maximum(block_count[e], 1)
    tile = (local // nb).astype(jnp.int32)
    row = local % nb
    blk = (block_start[e] + row).astype(jnp.int32)
    live = (blk < used_blocks).astype(jnp.int32)
    first = (row == 0).astype(jnp.int32)
    nxt = s + nb
    nxt = jnp.where(nxt < steps, nxt, -1).astype(jnp.int32)
    return blk, tile, e, live, first, nxt


def _moe_glu(sched, xg, wg, wu, tn=512):
    cap, half = xg.shape
    _, d, n = wg.shape
    assert d == 2 * half
    steps = sched[0].shape[0]
    return pl.pallas_call(
        _moe_glu_kernel,
        out_shape=jax.ShapeDtypeStruct((cap, n), BF16),
        grid_spec=pltpu.PrefetchScalarGridSpec(
            num_scalar_prefetch=len(sched),
            grid=(steps,),
            in_specs=[
                pl.BlockSpec((MOE_BLOCK, half), lambda s, b, *_: (b[s], 0)),
                pl.BlockSpec(memory_space=pl.ANY),
                pl.BlockSpec(memory_space=pl.ANY),
            ],
            out_specs=pl.BlockSpec((MOE_BLOCK, tn), lambda s, b, t, *_: (b[s], t[s])),
            scratch_shapes=[pltpu.VMEM((d, tn), F32), pltpu.VMEM((d, tn), F32),
                            pltpu.VMEM((d, tn), BF16), pltpu.VMEM((d, tn), BF16),
                            pltpu.SemaphoreType.DMA((2,))]),
        compiler_params=_params(1),
        name="moe_glu",
    )(*sched, xg, wg, wu)


def _moe_down(sched, a, wd, tn=1024):
    cap, k = a.shape
    n = wd.shape[2]
    steps = sched[0].shape[0]
    return pl.pallas_call(
        _moe_down_kernel,
        out_shape=jax.ShapeDtypeStruct((cap, n), F32),
        grid_spec=pltpu.PrefetchScalarGridSpec(
            num_scalar_prefetch=len(sched),
            grid=(steps,),
            in_specs=[
                pl.BlockSpec((MOE_BLOCK, k), lambda s, b, *_: (b[s], 0)),
                pl.BlockSpec(memory_space=pl.ANY),
            ],
            out_specs=pl.BlockSpec((MOE_BLOCK, tn), lambda s, b, t, *_: (b[s], t[s])),
            scratch_shapes=[pltpu.VMEM((k, tn), F32), pltpu.VMEM((k, tn), BF16),
                            pltpu.SemaphoreType.DMA((1,))]),
        compiler_params=_params(1),
        name="moe_down",
    )(*sched, a, wd)


def _combine_kernel(pos_ref, x_ref, info_ref, g_ref, fg_ref, y_hbm, o_ref, buf, sem,
                    *, rows, n_tok):
    i = pl.program_id(0)
    n = pl.num_programs(0)

    def issue(blk, slot):
        def body(r, carry):
            for k in range(2):
                row = pos_ref[k * n_tok + blk * rows + r]
                _row_copy(y_hbm, row, buf.at[slot, k], r, sem.at[slot]).start(priority=k)
            return carry
        lax.fori_loop(0, rows, body, 0, unroll=4)

    @pl.when(i == 0)
    def _():
        issue(0, 0)

    @pl.when(i + 1 < n)
    def _():
        issue(i + 1, (i + 1) % 2)

    slot = i % 2

    def wait_body(r, carry):
        for k in range(2):
            _row_copy(y_hbm, 0, buf.at[slot, k], r, sem.at[slot]).wait()
        return carry
    lax.fori_loop(0, rows, wait_body, 0, unroll=4)

    info = info_ref[...]
    f = info[:, 2:3] * buf[slot, 0] + info[:, 3:4] * buf[slot, 1]
    xn = x_ref[...] + g_ref[0] * f
    ms = jnp.mean(xn * xn, axis=-1, keepdims=True)
    o_ref[...] = xn * lax.rsqrt(ms + RMS_EPS) * fg_ref[...]


def _combine_final_norm(pos, x2, info, gate, final_g, y, seq, rows=256):
    t, d = x2.shape
    tpb = seq // rows
    return pl.pallas_call(
        functools.partial(_combine_kernel, rows=rows, n_tok=t),
        out_shape=jax.ShapeDtypeStruct((t, d), F32),
        grid_spec=pltpu.PrefetchScalarGridSpec(
            num_scalar_prefetch=1,
            grid=(t // rows,),
            in_specs=[
                pl.BlockSpec((rows, d), lambda i, p: (i, 0)),
                pl.BlockSpec((rows, LANES), lambda i, p: (i, 0)),
                pl.BlockSpec((1, 1, d), lambda i, p: (i // tpb, 0, 0)),
                pl.BlockSpec((1, d), lambda i, p: (0, 0)),
                pl.BlockSpec(memory_space=pl.ANY),
            ],
            out_specs=pl.BlockSpec((rows, d), lambda i, p: (i, 0)),
            scratch_shapes=[pltpu.VMEM((2, 2, rows, d), F32),
                            pltpu.SemaphoreType.DMA((2,))]),
        compiler_params=_params(1),
        name="moe_combine_final_norm",
    )(pos, x2, info, gate, final_g.reshape(1, d), y)


def _moe_layer_and_final_norm(x2, g, shift, scale, gate, router_w, wg, wu, wd,
                              final_g, seq):
    t, d = x2.shape
    rw_pad = jnp.pad(router_w, ((0, 0), (0, LANES - N_EXPERTS)))
    hp, info, cnt = _norm_route(x2, g, shift, scale, rw_pad, seq)

    e_tk = info[:, 0:2].astype(jnp.int32)
    rank_tk = info[:, 4:6].astype(jnp.int32)
    n_assign = e_tk.size
    counts = cnt[0, :N_EXPERTS].astype(jnp.int32)
    padded = (counts + MOE_BLOCK - 1) // MOE_BLOCK * MOE_BLOCK
    pend = jnp.cumsum(padded)
    pstart = pend - padded
    seg_start = jnp.sum(
        jnp.where(e_tk[..., None] == jnp.arange(N_EXPERTS, dtype=jnp.int32), pstart, 0), axis=-1)
    dest = (seg_start + rank_tk).astype(jnp.int32).reshape(-1)
    n_blocks = -(-n_assign // MOE_BLOCK) + N_EXPERTS
    cap = n_blocks * MOE_BLOCK
    tok_flat = jnp.arange(n_assign, dtype=jnp.int32) // 2
    tok_buf = jnp.full((cap,), t - 1, jnp.int32).at[dest].set(tok_flat)

    block_start = (pstart // MOE_BLOCK).astype(jnp.int32)
    block_count = (padded // MOE_BLOCK).astype(jnp.int32)
    used_blocks = pend[-1] // MOE_BLOCK
    block_count = block_count.at[N_EXPERTS - 1].set(n_blocks - block_start[N_EXPERTS - 1])

    xg = _dispatch(tok_buf, hp)

    def sched(n_tiles):
        return _grouped_schedule(block_start, block_count, used_blocks, n_blocks, n_tiles)

    tn_glu, tn_down = 512, 1024
    hmid = _moe_glu(sched(wg.shape[2] // tn_glu), xg, wg, wu, tn=tn_glu)
    y = _moe_down(sched(wd.shape[2] // tn_down), hmid, wd, tn=tn_down)

    pos = dest.reshape(t, 2).T.reshape(-1)
    return _combine_final_norm(pos, x2, info, gate, final_g, y, seq)


def kernel(x, c, norm_mix_g, norm_ffn_g, w_ada, b_ada, w_in, w_out, sc_conv_w, cf_conv_w,
           cf_conv_b, cf_ln_g, cf_ln_b, ffn_w_gate, ffn_w_up, ffn_w_down, router_w,
           moe_w_gate, moe_w_up, moe_w_down, final_g):
    bsz, seq, d = x.shape
    depth = w_ada.shape[0]
    assert depth == 2 and d == D_MODEL
    t = bsz * seq
    x2 = x.reshape(t, d)

    c_pad = jnp.pad(c, ((0, MOD_ROWS - bsz), (0, 0)))
    mod = _adaln_mod(c_pad, w_ada, b_ada)
    mod = mod[:, :bsz].reshape(depth, bsz, N_MOD, 1, d)

    ffn_wg_b = ffn_w_gate.astype(BF16)
    ffn_wu_b = ffn_w_up.astype(BF16)
    ffn_wd_b = ffn_w_down.astype(BF16)

    out = None
    for l in range(depth):
        sh1, sc1, g1, sh2, sc2, g2 = (mod[l, :, m] for m in range(N_MOD))

        h = _norm_modulate(x2, norm_mix_g[l], sh1, sc1, seq)
        y_sc, conv = _proj_mixers(h, w_in, l, sc_conv_w[l], cf_conv_w[l],
                                  cf_conv_b[l], seq)
        y_cf = _ln_silu(conv, cf_ln_g[l], cf_ln_b[l])
        x2 = _out_proj_residual(y_sc, y_cf, w_out, l, x2, g1, seq)

        i = l // 2
        if l % 2 == 0:
            h = _norm_modulate(x2, norm_ffn_g[l], sh2, sc2, seq)
            hmid = _glu(h, ffn_wg_b, ffn_wu_b, i)
            x2 = _down_residual(hmid, ffn_wd_b, i, x2, g2, seq)
        else:
            out = _moe_layer_and_final_norm(
                x2, norm_ffn_g[l], sh2, sc2, g2, router_w[i],
                moe_w_gate[i], moe_w_up[i], moe_w_down[i], final_g, seq)
    return out.reshape(bsz, seq, d)
```

```python
import functools

import jax
import jax.numpy as jnp
from jax import lax
from jax.experimental import pallas as pl
from jax.experimental.pallas import tpu as pltpu

F32 = jnp.float32
BF16 = jnp.bfloat16

D_MODEL = 4096
D_SC = 2048
D_CF = 2048
D_IN = 3 * D_SC + 2 * D_CF
SC_WIDTH = 3
CF_WIDTH = 31
N_EXPERTS = 8
D_FF_EXPERT = 4096
MOE_BLOCK = 512
N_MOD = 6
RMS_EPS = 1e-6
LN_EPS = 1e-5

V7X_VMEM_LIMIT = 56 * 1024 * 1024
LANES = 128
SUBLANES = 8
MOD_ROWS = 8


def _params(n_axes, vmem=V7X_VMEM_LIMIT, flags=None):
    return pltpu.CompilerParams(
        dimension_semantics=("arbitrary",) * n_axes, vmem_limit_bytes=vmem, flags=flags)


def _silu(v):
    return v * jax.nn.sigmoid(v)


def _mod_kernel(c_ref, w_ref, b_ref, o_ref):
    c = c_ref[...]
    acc = jnp.dot(_silu(c).astype(BF16), w_ref[0].astype(BF16),
                  preferred_element_type=F32)
    o_ref[0] = acc + b_ref[0]


def _adaln_mod(c_pad, w_ada, b_ada, tn=512):
    depth, d, n = w_ada.shape
    return pl.pallas_call(
        _mod_kernel,
        out_shape=jax.ShapeDtypeStruct((depth, MOD_ROWS, n), F32),
        grid=(depth, n // tn),
        in_specs=[
            pl.BlockSpec((MOD_ROWS, d), lambda l, j: (0, 0)),
            pl.BlockSpec((1, d, tn), lambda l, j: (l, 0, j)),
            pl.BlockSpec((1, 1, tn), lambda l, j: (l, 0, j)),
        ],
        out_specs=pl.BlockSpec((1, MOD_ROWS, tn), lambda l, j: (l, 0, j)),
        compiler_params=_params(2),
        name="adaln_mod",
    )(c_pad, w_ada, b_ada.reshape(depth, 1, n))


def _norm_mod(x, g_ref, sh_ref, sc_ref):
    ms = jnp.mean(x * x, axis=-1, keepdims=True)
    y = x * lax.rsqrt(ms + RMS_EPS) * g_ref[...]
    return y * (1.0 + sc_ref[0]) + sh_ref[0]


def _norm_mod_kernel(x_ref, g_ref, sh_ref, sc_ref, o_ref):
    o_ref[...] = _norm_mod(x_ref[...], g_ref, sh_ref, sc_ref).astype(o_ref.dtype)


def _pack_bf16_pair(lo, hi):
    def rounded_bits(v):
        b = pltpu.bitcast(v, jnp.uint32)
        return b + jnp.uint32(0x7FFF) + ((b >> 16) & jnp.uint32(1))
    return (rounded_bits(hi) & jnp.uint32(0xFFFF0000)) | (rounded_bits(lo) >> 16)


def _unpack_bf16_pair(w):
    lo = pltpu.bitcast(w << 16, F32).astype(BF16)
    hi = pltpu.bitcast(w & jnp.uint32(0xFFFF0000), F32).astype(BF16)
    return lo, hi


def _norm_route_kernel(x_ref, g_ref, sh_ref, sc_ref, rw_ref, hp_ref, info_ref, cnt_ref):
    @pl.when(pl.program_id(0) == 0)
    def _():
        cnt_ref[...] = jnp.zeros(cnt_ref.shape, F32)

    h = _norm_mod(x_ref[...], g_ref, sh_ref, sc_ref)
    half = h.shape[1] // 2
    hp_ref[...] = _pack_bf16_pair(h[:, :half], h[:, half:])
    rw = rw_ref[...]
    h_hi = h.astype(BF16)
    h_lo = (h - h_hi.astype(F32)).astype(BF16)
    rw_hi = rw.astype(BF16)
    rw_lo = (rw - rw_hi.astype(F32)).astype(BF16)
    logits = (jnp.dot(h_hi, rw_hi, preferred_element_type=F32)
              + jnp.dot(h_lo, rw_hi, preferred_element_type=F32)
              + jnp.dot(h_hi, rw_lo, preferred_element_type=F32))
    lane = lax.broadcasted_iota(jnp.int32, logits.shape, 1)
    neg = jnp.float32(-jnp.inf)
    lg = jnp.where(lane < N_EXPERTS, logits, neg)
    m1 = jnp.max(lg, axis=-1, keepdims=True)
    i1 = jnp.min(jnp.where(lg == m1, lane, LANES), axis=-1, keepdims=True)
    lg2 = jnp.where(lane == i1, neg, lg)
    m2 = jnp.max(lg2, axis=-1, keepdims=True)
    i2 = jnp.min(jnp.where(lg2 == m2, lane, LANES), axis=-1, keepdims=True)
    e2 = jnp.exp(m2 - m1)
    den = 1.0 + e2

    oh1 = (lane == i1).astype(F32)
    oh2 = (lane == i2).astype(F32)
    oh = oh1 + oh2
    tm = oh.shape[0]
    tri = (lax.broadcasted_iota(jnp.int32, (tm, tm), 1)
           < lax.broadcasted_iota(jnp.int32, (tm, tm), 0)).astype(BF16)
    before = jnp.dot(tri, oh.astype(BF16), preferred_element_type=F32) + cnt_ref[...]
    rank1 = jnp.sum(oh1 * before, axis=-1, keepdims=True)
    rank2 = jnp.sum(oh2 * before, axis=-1, keepdims=True)
    cnt_ref[...] = cnt_ref[...] + jnp.sum(oh, axis=0, keepdims=True)

    info = jnp.where(lane == 0, i1.astype(F32),
           jnp.where(lane == 1, i2.astype(F32),
           jnp.where(lane == 2, 1.0 / den,
           jnp.where(lane == 3, e2 / den,
           jnp.where(lane == 4, rank1,
           jnp.where(lane == 5, rank2, 0.0))))))
    info_ref[...] = info


def _row_specs(tm, d, tiles_per_batch):
    return [
        pl.BlockSpec((tm, d), lambda i: (i, 0)),
        pl.BlockSpec((1, d), lambda i: (0, 0)),
        pl.BlockSpec((1, 1, d), lambda i: (i // tiles_per_batch, 0, 0)),
        pl.BlockSpec((1, 1, d), lambda i: (i // tiles_per_batch, 0, 0)),
    ]


def _norm_modulate(x2, g, shift, scale, seq, tm=512):
    t, d = x2.shape
    return pl.pallas_call(
        _norm_mod_kernel,
        out_shape=jax.ShapeDtypeStruct((t, d), BF16),
        grid=(t // tm,),
        in_specs=_row_specs(tm, d, seq // tm),
        out_specs=pl.BlockSpec((tm, d), lambda i: (i, 0)),
        compiler_params=_params(1),
        name="norm_modulate",
    )(x2, g.reshape(1, d), shift, scale)


def _norm_route(x2, g, shift, scale, rw_pad, seq, tm=256):
    t, d = x2.shape
    return pl.pallas_call(
        _norm_route_kernel,
        out_shape=(jax.ShapeDtypeStruct((t, d // 2), jnp.uint32),
                   jax.ShapeDtypeStruct((t, LANES), F32),
                   jax.ShapeDtypeStruct((1, LANES), F32)),
        grid=(t // tm,),
        in_specs=_row_specs(tm, d, seq // tm) + [
            pl.BlockSpec((d, LANES), lambda i: (0, 0))],
        out_specs=(pl.BlockSpec((tm, d // 2), lambda i: (i, 0)),
                   pl.BlockSpec((tm, LANES), lambda i: (i, 0)),
                   pl.BlockSpec((1, LANES), lambda i: (0, 0))),
        compiler_params=_params(1),
        name="norm_route",
    )(x2, g.reshape(1, d), shift, scale, rw_pad)


def _out_proj_kernel(a1_ref, a2_ref, w_hbm, x_ref, g_ref, o_ref, stage_s, w_s, sem, *, layer):
    j = pl.program_id(0)
    i = pl.program_id(1)
    tn = stage_s.shape[1]

    def weight_copy(tile):
        col = pl.multiple_of(tile * tn, tn)
        return pltpu.make_async_copy(w_hbm.at[layer, :, pl.ds(col, tn)], stage_s, sem)

    @pl.when((j == 0) & (i == 0))
    def _():
        weight_copy(0).start()

    @pl.when(i == 0)
    def _():
        weight_copy(j).wait()
        w_s[...] = stage_s[...].astype(BF16)

        @pl.when(j + 1 < pl.num_programs(0))
        def _():
            weight_copy(j + 1).start()

    k1 = a1_ref.shape[1]
    acc = jnp.dot(a1_ref[...], w_s[:k1, :], preferred_element_type=F32)
    acc = acc + jnp.dot(a2_ref[...], w_s[k1:, :], preferred_element_type=F32)
    o_ref[...] = x_ref[...] + g_ref[0] * acc


def _out_proj_residual(y_sc, y_cf, w_out, layer, x2, gate, seq, tm=1024, tn=512):
    m, k1 = y_sc.shape
    k2 = y_cf.shape[1]
    n = w_out.shape[2]
    assert k1 + k2 == w_out.shape[1]
    tpb = seq // tm
    return pl.pallas_call(
        functools.partial(_out_proj_kernel, layer=layer),
        out_shape=jax.ShapeDtypeStruct((m, n), F32),
        grid=(n // tn, m // tm),
        in_specs=[pl.BlockSpec((tm, k1), lambda j, i: (i, 0)),
                  pl.BlockSpec((tm, k2), lambda j, i: (i, 0)),
                  pl.BlockSpec(memory_space=pl.ANY),
                  pl.BlockSpec((tm, tn), lambda j, i: (i, j)),
                  pl.BlockSpec((1, 1, tn), lambda j, i: (i // tpb, 0, j))],
        out_specs=pl.BlockSpec((tm, tn), lambda j, i: (i, j)),
        scratch_shapes=[pltpu.VMEM((k1 + k2, tn), F32),
                        pltpu.VMEM((k1 + k2, tn), BF16),
                        pltpu.SemaphoreType.DMA],
        compiler_params=_params(2),
        name="out_proj_residual",
    )(y_sc, y_cf, w_out, x2, gate)


def _glu_kernel(a_ref, wg_ref, wu_ref, o_ref):
    a = a_ref[...]
    p = jnp.dot(a, wg_ref[0], preferred_element_type=F32)
    q = jnp.dot(a, wu_ref[0], preferred_element_type=F32)
    o_ref[...] = (_silu(p) * q).astype(o_ref.dtype)


def _glu(a, wg, wu, layer, tm=2048, tn=256):
    m, k = a.shape
    n = wg.shape[2]
    assert n % tn == 0
    return pl.pallas_call(
        _glu_kernel,
        out_shape=jax.ShapeDtypeStruct((m, n), BF16),
        grid=(m // tm, n // tn),
        in_specs=[pl.BlockSpec((tm, k), lambda i, j: (i, 0)),
                  pl.BlockSpec((1, k, tn), lambda i, j: (layer, 0, j)),
                  pl.BlockSpec((1, k, tn), lambda i, j: (layer, 0, j))],
        out_specs=pl.BlockSpec((tm, tn), lambda i, j: (i, j)),
        compiler_params=_params(2),
        name="glu",
    )(a, wg, wu)


def _down_residual_kernel(a_ref, w_ref, x_ref, g_ref, o_ref):
    acc = jnp.dot(a_ref[...], w_ref[0], preferred_element_type=F32)
    o_ref[...] = x_ref[...] + g_ref[0] * acc


def _down_residual(a, w, layer, x2, gate, seq, tm=512, tn=512):
    m, k = a.shape
    n = w.shape[2]
    tpb = seq // tm
    return pl.pallas_call(
        _down_residual_kernel,
        out_shape=jax.ShapeDtypeStruct((m, n), F32),
        grid=(m // tm, n // tn),
        in_specs=[pl.BlockSpec((tm, k), lambda i, j: (i, 0)),
                  pl.BlockSpec((1, k, tn), lambda i, j: (layer, 0, j)),
                  pl.BlockSpec((tm, tn), lambda i, j: (i, j)),
                  pl.BlockSpec((1, 1, tn), lambda i, j: (i // tpb, 0, j))],
        out_specs=pl.BlockSpec((tm, tn), lambda i, j: (i, j)),
        compiler_params=_params(2),
        name="down_residual",
    )(a, w, x2, gate)


SC_HALO = SUBLANES
CF_HALO = 32
MIX_TC = 256


N_PROJ = 5


def _proj_mixer_kernel(h_ref, w_hbm, scw_ref, cfw_ref, cfb_ref, ysc_ref, conv_ref,
                       stage_s, w_s, cv_s, u_s, sh_s, sem, *, layer, tm, tiles_per_seq):
    j = pl.program_id(0)
    i = pl.program_id(1)
    n_chunks = pl.num_programs(0)

    def weight_copies(chunk):
        return [pltpu.make_async_copy(
            w_hbm.at[layer, :, pl.ds(pl.multiple_of((n * n_chunks + chunk) * MIX_TC, MIX_TC),
                                     MIX_TC)],
            stage_s.at[n], sem.at[n]) for n in range(N_PROJ)]

    @pl.when((j == 0) & (i == 0))
    def _():
        for c in weight_copies(0):
            c.start()

    @pl.when(i == 0)
    def _():
        for c in weight_copies(j):
            c.wait()
        for n in range(N_PROJ):
            w_s[n] = stage_s[n].astype(BF16)

        @pl.when(j + 1 < n_chunks)
        def _():
            for c in weight_copies(j + 1):
                c.start()

    seq_start = (i % tiles_per_seq) == 0
    h = h_ref[...]

    def proj(n):
        return jnp.dot(h, w_s[n], preferred_element_type=F32)

    @pl.when(seq_start)
    def _():
        u_s[0:CF_HALO, :] = jnp.zeros((CF_HALO, MIX_TC), F32)
        cv_s[0:SC_HALO, :] = jnp.zeros((SC_HALO, MIX_TC), F32)

    u_s[CF_HALO:, :] = proj(3) * jax.nn.sigmoid(proj(4))
    cv_s[SC_HALO:, :] = proj(1) * proj(2)
    b_gate = proj(0)
    span = tm + CF_HALO - SUBLANES
    for r in range(1, SUBLANES):
        sh_s[r - 1] = u_s[r:r + span, :]
    cfw = cfw_ref[...]
    acc = jnp.broadcast_to(cfb_ref[...], (tm, MIX_TC))
    base = CF_HALO - (CF_WIDTH - 1)
    for k in range(CF_WIDTH):
        q, r = divmod(base + k, SUBLANES)
        if r == 0:
            tap = u_s[q * SUBLANES:q * SUBLANES + tm, :]
        else:
            tap = sh_s[r - 1, q * SUBLANES:q * SUBLANES + tm, :]
        acc = acc + cfw[k:k + 1] * tap
    conv_ref[...] = acc
    u_s[0:CF_HALO, :] = u_s[tm:tm + CF_HALO, :]

    scw = scw_ref[...]
    y = scw[0:1] * cv_s[SC_HALO - 2:SC_HALO - 2 + tm, :]
    y = y + scw[1:2] * cv_s[SC_HALO - 1:SC_HALO - 1 + tm, :]
    y = y + scw[2:3] * cv_s[SC_HALO:SC_HALO + tm, :]
    ysc_ref[...] = (b_gate * y).astype(ysc_ref.dtype)
    cv_s[0:SC_HALO, :] = cv_s[tm:tm + SC_HALO, :]


def _ln_silu_kernel(u_ref, g_ref, b_ref, o_ref):
    u = u_ref[...]
    mu = jnp.mean(u, axis=-1, keepdims=True)
    uc = u - mu
    var = jnp.mean(uc * uc, axis=-1, keepdims=True)
    yn = uc * lax.rsqrt(var + LN_EPS) * g_ref[...] + b_ref[...]
    o_ref[...] = _silu(yn).astype(o_ref.dtype)


def _proj_mixers(h, w_in, layer, sc_w, cf_w, cf_b, seq, tm=512):
    t, d = h.shape
    tc = MIX_TC
    nch = D_SC // tc
    assert D_SC == D_CF and w_in.shape[2] == N_PROJ * D_SC and seq % tm == 0

    kern = functools.partial(_proj_mixer_kernel, layer=layer, tm=tm,
                             tiles_per_seq=seq // tm)
    return pl.pallas_call(
        kern,
        out_shape=(jax.ShapeDtypeStruct((t, D_SC), BF16),
                   jax.ShapeDtypeStruct((t, D_CF), F32)),
        grid=(nch, t // tm),
        in_specs=[
            pl.BlockSpec((tm, d), lambda j, i: (i, 0)),
            pl.BlockSpec(memory_space=pl.ANY),
            pl.BlockSpec((SC_WIDTH, tc), lambda j, i: (0, j)),
            pl.BlockSpec((CF_WIDTH, tc), lambda j, i: (0, j)),
            pl.BlockSpec((1, tc), lambda j, i: (0, j)),
        ],
        out_specs=(pl.BlockSpec((tm, tc), lambda j, i: (i, j)),
                   pl.BlockSpec((tm, tc), lambda j, i: (i, j))),
        scratch_shapes=[pltpu.VMEM((N_PROJ, d, tc), F32),
                        pltpu.VMEM((N_PROJ, d, tc), BF16),
                        pltpu.VMEM((tm + SC_HALO, tc), F32),
                        pltpu.VMEM((tm + CF_HALO, tc), F32),
                        pltpu.VMEM((SUBLANES - 1, tm + CF_HALO - SUBLANES, tc), F32),
                        pltpu.SemaphoreType.DMA((N_PROJ,))],
        compiler_params=_params(2, vmem=62 * 1024 * 1024),
        name="proj_mixers",
    )(h, w_in, sc_w, cf_w, cf_b.reshape(1, D_CF))


def _ln_silu(u, ln_g, ln_b, tm=512):
    t, n = u.shape
    return pl.pallas_call(
        _ln_silu_kernel,
        out_shape=jax.ShapeDtypeStruct((t, n), BF16),
        grid=(t // tm,),
        in_specs=[pl.BlockSpec((tm, n), lambda i: (i, 0)),
                  pl.BlockSpec((1, n), lambda i: (0, 0)),
                  pl.BlockSpec((1, n), lambda i: (0, 0))],
        out_specs=pl.BlockSpec((tm, n), lambda i: (i, 0)),
        compiler_params=_params(1),
        name="ln_silu",
    )(u, ln_g.reshape(1, n), ln_b.reshape(1, n))


def _row_copy(src_hbm, row, dst, r, sem):
    return pltpu.make_async_copy(src_hbm.at[pl.ds(row, 1)], dst.at[pl.ds(r, 1)], sem)


def _dispatch_kernel(tok_ref, src_hbm, o_ref, sem, *, rows):
    base = pl.program_id(0) * rows

    def issue(r, carry):
        _row_copy(src_hbm, tok_ref[base + r], o_ref, r, sem).start()
        return carry
    lax.fori_loop(0, rows, issue, 0, unroll=8)

    def drain(r, carry):
        _row_copy(src_hbm, 0, o_ref, r, sem).wait()
        return carry
    lax.fori_loop(0, rows, drain, 0, unroll=8)


def _dispatch(tok_buf, hp, rows=MOE_BLOCK):
    cap = tok_buf.shape[0]
    width = hp.shape[1]
    return pl.pallas_call(
        functools.partial(_dispatch_kernel, rows=rows),
        out_shape=jax.ShapeDtypeStruct((cap, width), hp.dtype),
        grid_spec=pltpu.PrefetchScalarGridSpec(
            num_scalar_prefetch=1,
            grid=(cap // rows,),
            in_specs=[pl.BlockSpec(memory_space=pl.ANY)],
            out_specs=pl.BlockSpec((rows, width), lambda i, tok: (i, 0)),
            scratch_shapes=[pltpu.SemaphoreType.DMA]),
        compiler_params=_params(1),
        name="moe_dispatch",
    )(tok_buf, hp)


def _weight_tile_copy(w_hbm, e, t, stage, sem):
    tn = stage.shape[1]
    col = pl.multiple_of(t * tn, tn)
    return pltpu.make_async_copy(w_hbm.at[e, :, pl.ds(col, tn)], stage, sem)


def _refresh_weights(s, tile_ref, exp_ref, first_ref, next_ref, w_hbms, stages, w_bf16s, sem):
    def copies(step):
        return [_weight_tile_copy(w, exp_ref[step], tile_ref[step], st, sem.at[n])
                for n, (w, st) in enumerate(zip(w_hbms, stages))]

    @pl.when(s == 0)
    def _():
        for c in copies(0):
            c.start()

    @pl.when(first_ref[s] == 1)
    def _():
        for c in copies(s):
            c.wait()
        for st, wb in zip(stages, w_bf16s):
            wb[...] = st[...].astype(BF16)
        nxt = next_ref[s]

        @pl.when(nxt >= 0)
        def _():
            for c in copies(nxt):
                c.start()


def _moe_glu_kernel(blk_ref, tile_ref, exp_ref, live_ref, first_ref, next_ref,
                    x_ref, wg_hbm, wu_hbm, o_ref, stage_g, stage_u, wg_s, wu_s, sem):
    s = pl.program_id(0)
    _refresh_weights(s, tile_ref, exp_ref, first_ref, next_ref,
                     (wg_hbm, wu_hbm), (stage_g, stage_u), (wg_s, wu_s), sem)

    @pl.when(live_ref[s] == 1)
    def _():
        x = jnp.concatenate(_unpack_bf16_pair(x_ref[...]), axis=-1)
        p = jnp.dot(x, wg_s[...], preferred_element_type=F32)
        q = jnp.dot(x, wu_s[...], preferred_element_type=F32)
        o_ref[...] = (_silu(p) * q).astype(o_ref.dtype)

    @pl.when(live_ref[s] == 0)
    def _():
        o_ref[...] = jnp.zeros_like(o_ref)


def _moe_down_kernel(blk_ref, tile_ref, exp_ref, live_ref, first_ref, next_ref,
                     a_ref, w_hbm, o_ref, stage, w_s, sem):
    s = pl.program_id(0)
    _refresh_weights(s, tile_ref, exp_ref, first_ref, next_ref,
                     (w_hbm,), (stage,), (w_s,), sem)

    @pl.when(live_ref[s] == 1)
    def _():
        o_ref[...] = jnp.dot(a_ref[...], w_s[...], preferred_element_type=F32)

    @pl.when(live_ref[s] == 0)
    def _():
        o_ref[...] = jnp.zeros_like(o_ref)


def _grouped_schedule(block_start, block_count, used_blocks, n_blocks, n_tiles):
    steps = n_blocks * n_tiles
    step_end = (block_start + block_count) * n_tiles
    s = jnp.arange(steps, dtype=jnp.int32)
    e = jnp.minimum(jnp.sum((s[:, None] >= step_end[None, :]).astype(jnp.int32), axis=1),
                    N_EXPERTS - 1)
    local = s - block_start[e] * n_tiles
    nb = jnp.maximum(block_count[e], 1)
    tile = (local // nb).astype(jnp.int32)
    row = local % nb
    blk = (block_start[e] + row).astype(jnp.int32)
    live = (blk < used_blocks).astype(jnp.int32)
    first = (row == 0).astype(jnp.int32)
    nxt = s + nb
    nxt = jnp.where(nxt < steps, nxt, -1).astype(jnp.int32)
    return blk, tile, e, live, first, nxt


def _moe_glu(sched, xg, wg, wu, tn=512):
    cap, half = xg.shape
    _, d, n = wg.shape
    assert d == 2 * half
    steps = sched[0].shape[0]
    return pl.pallas_call(
        _moe_glu_kernel,
        out_shape=jax.ShapeDtypeStruct((cap, n), BF16),
        grid_spec=pltpu.PrefetchScalarGridSpec(
            num_scalar_prefetch=len(sched),
            grid=(steps,),
            in_specs=[
                pl.BlockSpec((MOE_BLOCK, half), lambda s, b, *_: (b[s], 0)),
                pl.BlockSpec(memory_space=pl.ANY),
                pl.BlockSpec(memory_space=pl.ANY),
            ],
            out_specs=pl.BlockSpec((MOE_BLOCK, tn), lambda s, b, t, *_: (b[s], t[s])),
            scratch_shapes=[pltpu.VMEM((d, tn), F32), pltpu.VMEM((d, tn), F32),
                            pltpu.VMEM((d, tn), BF16), pltpu.VMEM((d, tn), BF16),
                            pltpu.SemaphoreType.DMA((2,))]),
        compiler_params=_params(1),
        name="moe_glu",
    )(*sched, xg, wg, wu)


def _moe_down(sched, a, wd, tn=1024):
    cap, k = a.shape
    n = wd.shape[2]
    steps = sched[0].shape[0]
    return pl.pallas_call(
        _moe_down_kernel,
        out_shape=jax.ShapeDtypeStruct((cap, n), F32),
        grid_spec=pltpu.PrefetchScalarGridSpec(
            num_scalar_prefetch=len(sched),
            grid=(steps,),
            in_specs=[
                pl.BlockSpec((MOE_BLOCK, k), lambda s, b, *_: (b[s], 0)),
                pl.BlockSpec(memory_space=pl.ANY),
            ],
            out_specs=pl.BlockSpec((MOE_BLOCK, tn), lambda s, b, t, *_: (b[s], t[s])),
            scratch_shapes=[pltpu.VMEM((k, tn), F32), pltpu.VMEM((k, tn), BF16),
                            pltpu.SemaphoreType.DMA((1,))]),
        compiler_params=_params(1),
        name="moe_down",
    )(*sched, a, wd)


def _combine_kernel(pos_ref, x_ref, info_ref, g_ref, fg_ref, y_hbm, o_ref, buf, sem,
                    *, rows, n_tok):
    i = pl.program_id(0)
    n = pl.num_programs(0)

    def issue(blk, slot):
        def body(r, carry):
            for k in range(2):
                row = pos_ref[k * n_tok + blk * rows + r]
                _row_copy(y_hbm, row, buf.at[slot, k], r, sem.at[slot]).start()
            return carry
        lax.fori_loop(0, rows, body, 0, unroll=4)

    @pl.when(i == 0)
    def _():
        issue(0, 0)

    @pl.when(i + 1 < n)
    def _():
        issue(i + 1, (i + 1) % 2)

    slot = i % 2

    def wait_body(r, carry):
        for k in range(2):
            _row_copy(y_hbm, 0, buf.at[slot, k], r, sem.at[slot]).wait()
        return carry
    lax.fori_loop(0, rows, wait_body, 0, unroll=4)

    info = info_ref[...]
    f = info[:, 2:3] * buf[slot, 0] + info[:, 3:4] * buf[slot, 1]
    xn = x_ref[...] + g_ref[0] * f
    ms = jnp.mean(xn * xn, axis=-1, keepdims=True)
    o_ref[...] = xn * lax.rsqrt(ms + RMS_EPS) * fg_ref[...]


def _combine_final_norm(pos, x2, info, gate, final_g, y, seq, rows=256):
    t, d = x2.shape
    tpb = seq // rows
    return pl.pallas_call(
        functools.partial(_combine_kernel, rows=rows, n_tok=t),
        out_shape=jax.ShapeDtypeStruct((t, d), F32),
        grid_spec=pltpu.PrefetchScalarGridSpec(
            num_scalar_prefetch=1,
            grid=(t // rows,),
            in_specs=[
                pl.BlockSpec((rows, d), lambda i, p: (i, 0)),
                pl.BlockSpec((rows, LANES), lambda i, p: (i, 0)),
                pl.BlockSpec((1, 1, d), lambda i, p: (i // tpb, 0, 0)),
                pl.BlockSpec((1, d), lambda i, p: (0, 0)),
                pl.BlockSpec(memory_space=pl.ANY),
            ],
            out_specs=pl.BlockSpec((rows, d), lambda i, p: (i, 0)),
            scratch_shapes=[pltpu.VMEM((2, 2, rows, d), F32),
                            pltpu.SemaphoreType.DMA((2,))]),
        compiler_params=_params(1),
        name="moe_combine_final_norm",
    )(pos, x2, info, gate, final_g.reshape(1, d), y)


def _moe_layer_and_final_norm(x2, g, shift, scale, gate, router_w, wg, wu, wd,
                              final_g, seq):
    t, d = x2.shape
    rw_pad = jnp.pad(router_w, ((0, 0), (0, LANES - N_EXPERTS)))
    hp, info, cnt = _norm_route(x2, g, shift, scale, rw_pad, seq)

    e_tk = info[:, 0:2].astype(jnp.int32)
    rank_tk = info[:, 4:6].astype(jnp.int32)
    n_assign = e_tk.size
    counts = cnt[0, :N_EXPERTS].astype(jnp.int32)
    padded = (counts + MOE_BLOCK - 1) // MOE_BLOCK * MOE_BLOCK
    pend = jnp.cumsum(padded)
    pstart = pend - padded
    seg_start = jnp.sum(
        jnp.where(e_tk[..., None] == jnp.arange(N_EXPERTS, dtype=jnp.int32), pstart, 0), axis=-1)
    dest = (seg_start + rank_tk).astype(jnp.int32).reshape(-1)
    n_blocks = -(-n_assign // MOE_BLOCK) + N_EXPERTS
    cap = n_blocks * MOE_BLOCK
    tok_flat = jnp.arange(n_assign, dtype=jnp.int32) // 2
    tok_buf = jnp.full((cap,), t - 1, jnp.int32).at[dest].set(tok_flat)

    block_start = (pstart // MOE_BLOCK).astype(jnp.int32)
    block_count = (padded // MOE_BLOCK).astype(jnp.int32)
    used_blocks = pend[-1] // MOE_BLOCK
    block_count = block_count.at[N_EXPERTS - 1].set(n_blocks - block_start[N_EXPERTS - 1])

    xg = _dispatch(tok_buf, hp)

    def sched(n_tiles):
        return _grouped_schedule(block_start, block_count, used_blocks, n_blocks, n_tiles)

    tn_glu, tn_down = 512, 1024
    hmid = _moe_glu(sched(wg.shape[2] // tn_glu), xg, wg, wu, tn=tn_glu)
    y = _moe_down(sched(wd.shape[2] // tn_down), hmid, wd, tn=tn_down)

    pos = dest.reshape(t, 2).T.reshape(-1)
    return _combine_final_norm(pos, x2, info, gate, final_g, y, seq)


def kernel(x, c, norm_mix_g, norm_ffn_g, w_ada, b_ada, w_in, w_out, sc_conv_w, cf_conv_w,
           cf_conv_b, cf_ln_g, cf_ln_b, ffn_w_gate, ffn_w_up, ffn_w_down, router_w,
           moe_w_gate, moe_w_up, moe_w_down, final_g):
    bsz, seq, d = x.shape
    depth = w_ada.shape[0]
    assert depth == 2 and d == D_MODEL
    t = bsz * seq
    x2 = x.reshape(t, d)

    c_pad = jnp.pad(c, ((0, MOD_ROWS - bsz), (0, 0)))
    mod = _adaln_mod(c_pad, w_ada, b_ada)
    mod = mod[:, :bsz].reshape(depth, bsz, N_MOD, 1, d)

    ffn_wg_b = ffn_w_gate.astype(BF16)
    ffn_wu_b = ffn_w_up.astype(BF16)
    ffn_wd_b = ffn_w_down.astype(BF16)

    out = None
    for l in range(depth):
        sh1, sc1, g1, sh2, sc2, g2 = (mod[l, :, m] for m in range(N_MOD))

        h = _norm_modulate(x2, norm_mix_g[l], sh1, sc1, seq)
        y_sc, conv = _proj_mixers(h, w_in, l, sc_conv_w[l], cf_conv_w[l],
                                  cf_conv_b[l], seq)
        y_cf = _ln_silu(conv, cf_ln_g[l], cf_ln_b[l])
        x2 = _out_proj_residual(y_sc, y_cf, w_out, l, x2, g1, seq)

        i = l // 2
        if l % 2 == 0:
            h = _norm_modulate(x2, norm_ffn_g[l], sh2, sc2, seq)
            hmid = _glu(h, ffn_wg_b, ffn_wu_b, i)
            x2 = _down_residual(hmid, ffn_wd_b, i, x2, g2, seq)
        else:
            out = _moe_layer_and_final_norm(
                x2, norm_ffn_g[l], sh2, sc2, g2, router_w[i],
                moe_w_gate[i], moe_w_up[i], moe_w_down[i], final_g, seq)
    return out.reshape(bsz, seq, d)
```

```python
import functools

import jax
import jax.numpy as jnp
from jax import lax
from jax.experimental import pallas as pl
from jax.experimental.pallas import tpu as pltpu

F32 = jnp.float32
BF16 = jnp.bfloat16

D_MODEL = 4096
D_SC = 2048
D_CF = 2048
D_IN = 3 * D_SC + 2 * D_CF
SC_WIDTH = 3
CF_WIDTH = 31
N_EXPERTS = 8
D_FF_EXPERT = 4096
MOE_BLOCK = 512
N_MOD = 6
RMS_EPS = 1e-6
LN_EPS = 1e-5

V7X_VMEM_LIMIT = 56 * 1024 * 1024
LANES = 128
SUBLANES = 8
MOD_ROWS = 8


def _params(n_axes, vmem=V7X_VMEM_LIMIT, flags=None):
    return pltpu.CompilerParams(
        dimension_semantics=("arbitrary",) * n_axes, vmem_limit_bytes=vmem, flags=flags)


def _silu(v):
    return v * jax.nn.sigmoid(v)


def _mod_kernel(c_ref, w_ref, b_ref, o_ref):
    c = c_ref[...]
    acc = jnp.dot(_silu(c).astype(BF16), w_ref[0].astype(BF16),
                  preferred_element_type=F32)
    o_ref[0] = acc + b_ref[0]


def _adaln_mod(c_pad, w_ada, b_ada, tn=512):
    depth, d, n = w_ada.shape
    return pl.pallas_call(
        _mod_kernel,
        out_shape=jax.ShapeDtypeStruct((depth, MOD_ROWS, n), F32),
        grid=(depth, n // tn),
        in_specs=[
            pl.BlockSpec((MOD_ROWS, d), lambda l, j: (0, 0)),
            pl.BlockSpec((1, d, tn), lambda l, j: (l, 0, j)),
            pl.BlockSpec((1, 1, tn), lambda l, j: (l, 0, j)),
        ],
        out_specs=pl.BlockSpec((1, MOD_ROWS, tn), lambda l, j: (l, 0, j)),
        compiler_params=_params(2),
        name="adaln_mod",
    )(c_pad, w_ada, b_ada.reshape(depth, 1, n))


def _norm_mod(x, g_ref, sh_ref, sc_ref):
    ms = jnp.mean(x * x, axis=-1, keepdims=True)
    y = x * lax.rsqrt(ms + RMS_EPS) * g_ref[...]
    return y * (1.0 + sc_ref[0]) + sh_ref[0]


def _norm_mod_kernel(x_ref, g_ref, sh_ref, sc_ref, o_ref):
    o_ref[...] = _norm_mod(x_ref[...], g_ref, sh_ref, sc_ref).astype(o_ref.dtype)


def _pack_bf16_pair(lo, hi):
    def rounded_bits(v):
        b = pltpu.bitcast(v, jnp.uint32)
        return b + jnp.uint32(0x7FFF) + ((b >> 16) & jnp.uint32(1))
    return (rounded_bits(hi) & jnp.uint32(0xFFFF0000)) | (rounded_bits(lo) >> 16)


def _unpack_bf16_pair(w):
    lo = pltpu.bitcast(w << 16, F32).astype(BF16)
    hi = pltpu.bitcast(w & jnp.uint32(0xFFFF0000), F32).astype(BF16)
    return lo, hi


def _norm_route_kernel(x_ref, g_ref, sh_ref, sc_ref, rw_ref, hp_ref, info_ref, cnt_ref):
    @pl.when(pl.program_id(0) == 0)
    def _():
        cnt_ref[...] = jnp.zeros(cnt_ref.shape, F32)

    h = _norm_mod(x_ref[...], g_ref, sh_ref, sc_ref)
    half = h.shape[1] // 2
    hp_ref[...] = _pack_bf16_pair(h[:, :half], h[:, half:])
    rw = rw_ref[...]
    h_hi = h.astype(BF16)
    h_lo = (h - h_hi.astype(F32)).astype(BF16)
    rw_hi = rw.astype(BF16)
    rw_lo = (rw - rw_hi.astype(F32)).astype(BF16)
    logits = (jnp.dot(h_hi, rw_hi, preferred_element_type=F32)
              + jnp.dot(h_lo, rw_hi, preferred_element_type=F32)
              + jnp.dot(h_hi, rw_lo, preferred_element_type=F32))
    lane = lax.broadcasted_iota(jnp.int32, logits.shape, 1)
    neg = jnp.float32(-jnp.inf)
    lg = jnp.where(lane < N_EXPERTS, logits, neg)
    m1 = jnp.max(lg, axis=-1, keepdims=True)
    i1 = jnp.min(jnp.where(lg == m1, lane, LANES), axis=-1, keepdims=True)
    lg2 = jnp.where(lane == i1, neg, lg)
    m2 = jnp.max(lg2, axis=-1, keepdims=True)
    i2 = jnp.min(jnp.where(lg2 == m2, lane, LANES), axis=-1, keepdims=True)
    e2 = jnp.exp(m2 - m1)
    den = 1.0 + e2

    oh1 = (lane == i1).astype(F32)
    oh2 = (lane == i2).astype(F32)
    oh = oh1 + oh2
    tm = oh.shape[0]
    tri = (lax.broadcasted_iota(jnp.int32, (tm, tm), 1)
           < lax.broadcasted_iota(jnp.int32, (tm, tm), 0)).astype(BF16)
    before = jnp.dot(tri, oh.astype(BF16), preferred_element_type=F32) + cnt_ref[...]
    rank1 = jnp.sum(oh1 * before, axis=-1, keepdims=True)
    rank2 = jnp.sum(oh2 * before, axis=-1, keepdims=True)
    cnt_ref[...] = cnt_ref[...] + jnp.sum(oh, axis=0, keepdims=True)

    info = jnp.where(lane == 0, i1.astype(F32),
           jnp.where(lane == 1, i2.astype(F32),
           jnp.where(lane == 2, 1.0 / den,
           jnp.where(lane == 3, e2 / den,
           jnp.where(lane == 4, rank1,
           jnp.where(lane == 5, rank2, 0.0))))))
    info_ref[...] = info


def _row_specs(tm, d, tiles_per_batch):
    return [
        pl.BlockSpec((tm, d), lambda i: (i, 0)),
        pl.BlockSpec((1, d), lambda i: (0, 0)),
        pl.BlockSpec((1, 1, d), lambda i: (i // tiles_per_batch, 0, 0)),
        pl.BlockSpec((1, 1, d), lambda i: (i // tiles_per_batch, 0, 0)),
    ]


def _norm_modulate(x2, g, shift, scale, seq, tm=512):
    t, d = x2.shape
    return pl.pallas_call(
        _norm_mod_kernel,
        out_shape=jax.ShapeDtypeStruct((t, d), BF16),
        grid=(t // tm,),
        in_specs=_row_specs(tm, d, seq // tm),
        out_specs=pl.BlockSpec((tm, d), lambda i: (i, 0)),
        compiler_params=_params(1),
        name="norm_modulate",
    )(x2, g.reshape(1, d), shift, scale)


def _norm_route(x2, g, shift, scale, rw_pad, seq, tm=256):
    t, d = x2.shape
    return pl.pallas_call(
        _norm_route_kernel,
        out_shape=(jax.ShapeDtypeStruct((t, d // 2), jnp.uint32),
                   jax.ShapeDtypeStruct((t, LANES), F32),
                   jax.ShapeDtypeStruct((1, LANES), F32)),
        grid=(t // tm,),
        in_specs=_row_specs(tm, d, seq // tm) + [
            pl.BlockSpec((d, LANES), lambda i: (0, 0))],
        out_specs=(pl.BlockSpec((tm, d // 2), lambda i: (i, 0)),
                   pl.BlockSpec((tm, LANES), lambda i: (i, 0)),
                   pl.BlockSpec((1, LANES), lambda i: (0, 0))),
        compiler_params=_params(1),
        name="norm_route",
    )(x2, g.reshape(1, d), shift, scale, rw_pad)


def _out_proj_kernel(a1_ref, a2_ref, w_hbm, x_ref, g_ref, o_ref, stage_s, w_s, sem, *, layer):
    j = pl.program_id(0)
    i = pl.program_id(1)
    tn = stage_s.shape[1]

    def weight_copy(tile):
        col = pl.multiple_of(tile * tn, tn)
        return pltpu.make_async_copy(w_hbm.at[layer, :, pl.ds(col, tn)], stage_s, sem)

    @pl.when((j == 0) & (i == 0))
    def _():
        weight_copy(0).start()

    @pl.when(i == 0)
    def _():
        weight_copy(j).wait()
        w_s[...] = stage_s[...].astype(BF16)

        @pl.when(j + 1 < pl.num_programs(0))
        def _():
            weight_copy(j + 1).start()

    k1 = a1_ref.shape[1]
    acc = jnp.dot(a1_ref[...], w_s[:k1, :], preferred_element_type=F32)
    acc = acc + jnp.dot(a2_ref[...], w_s[k1:, :], preferred_element_type=F32)
    o_ref[...] = x_ref[...] + g_ref[0] * acc


def _out_proj_residual(y_sc, y_cf, w_out, layer, x2, gate, seq, tm=1024, tn=512):
    m, k1 = y_sc.shape
    k2 = y_cf.shape[1]
    n = w_out.shape[2]
    assert k1 + k2 == w_out.shape[1]
    tpb = seq // tm
    return pl.pallas_call(
        functools.partial(_out_proj_kernel, layer=layer),
        out_shape=jax.ShapeDtypeStruct((m, n), F32),
        grid=(n // tn, m // tm),
        in_specs=[pl.BlockSpec((tm, k1), lambda j, i: (i, 0)),
                  pl.BlockSpec((tm, k2), lambda j, i: (i, 0)),
                  pl.BlockSpec(memory_space=pl.ANY),
                  pl.BlockSpec((tm, tn), lambda j, i: (i, j)),
                  pl.BlockSpec((1, 1, tn), lambda j, i: (i // tpb, 0, j))],
        out_specs=pl.BlockSpec((tm, tn), lambda j, i: (i, j)),
        scratch_shapes=[pltpu.VMEM((k1 + k2, tn), F32),
                        pltpu.VMEM((k1 + k2, tn), BF16),
                        pltpu.SemaphoreType.DMA],
        compiler_params=_params(2),
        name="out_proj_residual",
    )(y_sc, y_cf, w_out, x2, gate)


def _glu_kernel(a_ref, wg_ref, wu_ref, o_ref):
    a = a_ref[...]
    p = jnp.dot(a, wg_ref[0], preferred_element_type=F32)
    q = jnp.dot(a, wu_ref[0], preferred_element_type=F32)
    o_ref[...] = (_silu(p) * q).astype(o_ref.dtype)


def _glu(a, wg, wu, layer, tm=2048, tn=256):
    m, k = a.shape
    n = wg.shape[2]
    assert n % tn == 0
    return pl.pallas_call(
        _glu_kernel,
        out_shape=jax.ShapeDtypeStruct((m, n), BF16),
        grid=(m // tm, n // tn),
        in_specs=[pl.BlockSpec((tm, k), lambda i, j: (i, 0)),
                  pl.BlockSpec((1, k, tn), lambda i, j: (layer, 0, j)),
                  pl.BlockSpec((1, k, tn), lambda i, j: (layer, 0, j))],
        out_specs=pl.BlockSpec((tm, tn), lambda i, j: (i, j)),
        compiler_params=_params(2),
        name="glu",
    )(a, wg, wu)


def _down_residual_kernel(a_ref, w_ref, x_ref, g_ref, o_ref):
    acc = jnp.dot(a_ref[...], w_ref[0], preferred_element_type=F32)
    o_ref[...] = x_ref[...] + g_ref[0] * acc


def _down_residual(a, w, layer, x2, gate, seq, tm=512, tn=512):
    m, k = a.shape
    n = w.shape[2]
    tpb = seq // tm
    return pl.pallas_call(
        _down_residual_kernel,
        out_shape=jax.ShapeDtypeStruct((m, n), F32),
        grid=(m // tm, n // tn),
        in_specs=[pl.BlockSpec((tm, k), lambda i, j: (i, 0)),
                  pl.BlockSpec((1, k, tn), lambda i, j: (layer, 0, j)),
                  pl.BlockSpec((tm, tn), lambda i, j: (i, j)),
                  pl.BlockSpec((1, 1, tn), lambda i, j: (i // tpb, 0, j))],
        out_specs=pl.BlockSpec((tm, tn), lambda i, j: (i, j)),
        compiler_params=_params(2),
        name="down_residual",
    )(a, w, x2, gate)


SC_HALO = SUBLANES
CF_HALO = 32
MIX_TC = 256


N_PROJ = 5


def _proj_mixer_kernel(h_ref, w_hbm, scw_ref, cfw_ref, cfb_ref, ysc_ref, conv_ref,
                       stage_s, w_s, cv_s, u_s, sh_s, sem, *, layer, tm, tiles_per_seq):
    j = pl.program_id(0)
    i = pl.program_id(1)
    n_chunks = pl.num_programs(0)

    def weight_copies(chunk):
        return [pltpu.make_async_copy(
            w_hbm.at[layer, :, pl.ds(pl.multiple_of((n * n_chunks + chunk) * MIX_TC, MIX_TC),
                                     MIX_TC)],
            stage_s.at[n], sem.at[n]) for n in range(N_PROJ)]

    @pl.when((j == 0) & (i == 0))
    def _():
        for c in weight_copies(0):
            c.start()

    @pl.when(i == 0)
    def _():
        for c in weight_copies(j):
            c.wait()
        for n in range(N_PROJ):
            w_s[n] = stage_s[n].astype(BF16)

        @pl.when(j + 1 < n_chunks)
        def _():
            for c in weight_copies(j + 1):
                c.start()

    seq_start = (i % tiles_per_seq) == 0
    h = h_ref[...]

    def proj(n):
        return jnp.dot(h, w_s[n], preferred_element_type=F32)

    @pl.when(seq_start)
    def _():
        u_s[0:CF_HALO, :] = jnp.zeros((CF_HALO, MIX_TC), F32)
        cv_s[0:SC_HALO, :] = jnp.zeros((SC_HALO, MIX_TC), F32)

    u_s[CF_HALO:, :] = proj(3) * jax.nn.sigmoid(proj(4))
    cv_s[SC_HALO:, :] = proj(1) * proj(2)
    b_gate = proj(0)
    span = tm + CF_HALO - SUBLANES
    for r in range(1, SUBLANES):
        sh_s[r - 1] = u_s[r:r + span, :]
    cfw = cfw_ref[...]
    acc = jnp.broadcast_to(cfb_ref[...], (tm, MIX_TC))
    base = CF_HALO - (CF_WIDTH - 1)
    for k in range(CF_WIDTH):
        q, r = divmod(base + k, SUBLANES)
        if r == 0:
            tap = u_s[q * SUBLANES:q * SUBLANES + tm, :]
        else:
            tap = sh_s[r - 1, q * SUBLANES:q * SUBLANES + tm, :]
        acc = acc + cfw[k:k + 1] * tap
    conv_ref[...] = acc
    u_s[0:CF_HALO, :] = u_s[tm:tm + CF_HALO, :]

    scw = scw_ref[...]
    y = scw[0:1] * cv_s[SC_HALO - 2:SC_HALO - 2 + tm, :]
    y = y + scw[1:2] * cv_s[SC_HALO - 1:SC_HALO - 1 + tm, :]
    y = y + scw[2:3] * cv_s[SC_HALO:SC_HALO + tm, :]
    ysc_ref[...] = (b_gate * y).astype(ysc_ref.dtype)
    cv_s[0:SC_HALO, :] = cv_s[tm:tm + SC_HALO, :]


def _ln_silu_kernel(u_ref, g_ref, b_ref, o_ref):
    u = u_ref[...]
    mu = jnp.mean(u, axis=-1, keepdims=True)
    uc = u - mu
    var = jnp.mean(uc * uc, axis=-1, keepdims=True)
    yn = uc * lax.rsqrt(var + LN_EPS) * g_ref[...] + b_ref[...]
    o_ref[...] = _silu(yn).astype(o_ref.dtype)


def _proj_mixers(h, w_in, layer, sc_w, cf_w, cf_b, seq, tm=512):
    t, d = h.shape
    tc = MIX_TC
    nch = D_SC // tc
    assert D_SC == D_CF and w_in.shape[2] == N_PROJ * D_SC and seq % tm == 0

    kern = functools.partial(_proj_mixer_kernel, layer=layer, tm=tm,
                             tiles_per_seq=seq // tm)
    return pl.pallas_call(
        kern,
        out_shape=(jax.ShapeDtypeStruct((t, D_SC), BF16),
                   jax.ShapeDtypeStruct((t, D_CF), F32)),
        grid=(nch, t // tm),
        in_specs=[
            pl.BlockSpec((tm, d), lambda j, i: (i, 0)),
            pl.BlockSpec(memory_space=pl.ANY),
            pl.BlockSpec((SC_WIDTH, tc), lambda j, i: (0, j)),
            pl.BlockSpec((CF_WIDTH, tc), lambda j, i: (0, j)),
            pl.BlockSpec((1, tc), lambda j, i: (0, j)),
        ],
        out_specs=(pl.BlockSpec((tm, tc), lambda j, i: (i, j)),
                   pl.BlockSpec((tm, tc), lambda j, i: (i, j))),
        scratch_shapes=[pltpu.VMEM((N_PROJ, d, tc), F32),
                        pltpu.VMEM((N_PROJ, d, tc), BF16),
                        pltpu.VMEM((tm + SC_HALO, tc), F32),
                        pltpu.VMEM((tm + CF_HALO, tc), F32),
                        pltpu.VMEM((SUBLANES - 1, tm + CF_HALO - SUBLANES, tc), F32),
                        pltpu.SemaphoreType.DMA((N_PROJ,))],
        compiler_params=_params(2, vmem=62 * 1024 * 1024),
        name="proj_mixers",
    )(h, w_in, sc_w, cf_w, cf_b.reshape(1, D_CF))


def _ln_silu(u, ln_g, ln_b, tm=512):
    t, n = u.shape
    return pl.pallas_call(
        _ln_silu_kernel,
        out_shape=jax.ShapeDtypeStruct((t, n), BF16),
        grid=(t // tm,),
        in_specs=[pl.BlockSpec((tm, n), lambda i: (i, 0)),
                  pl.BlockSpec((1, n), lambda i: (0, 0)),
                  pl.BlockSpec((1, n), lambda i: (0, 0))],
        out_specs=pl.BlockSpec((tm, n), lambda i: (i, 0)),
        compiler_params=_params(1),
        name="ln_silu",
    )(u, ln_g.reshape(1, n), ln_b.reshape(1, n))


def _row_copy(src_hbm, row, dst, r, sem):
    return pltpu.make_async_copy(src_hbm.at[pl.ds(row, 1)], dst.at[pl.ds(r, 1)], sem)


N_DMA_PRIORITIES = 2


def _dispatch_kernel(tok_ref, src_hbm, o_ref, sem, *, rows):
    base = pl.program_id(0) * rows

    def issue(g, carry):
        for p in range(N_DMA_PRIORITIES):
            r = g * N_DMA_PRIORITIES + p
            _row_copy(src_hbm, tok_ref[base + r], o_ref, r, sem).start(priority=p)
        return carry
    lax.fori_loop(0, rows // N_DMA_PRIORITIES, issue, 0, unroll=4)

    def drain(r, carry):
        _row_copy(src_hbm, 0, o_ref, r, sem).wait()
        return carry
    lax.fori_loop(0, rows, drain, 0, unroll=8)


def _dispatch(tok_buf, hp, rows=MOE_BLOCK):
    cap = tok_buf.shape[0]
    width = hp.shape[1]
    return pl.pallas_call(
        functools.partial(_dispatch_kernel, rows=rows),
        out_shape=jax.ShapeDtypeStruct((cap, width), hp.dtype),
        grid_spec=pltpu.PrefetchScalarGridSpec(
            num_scalar_prefetch=1,
            grid=(cap // rows,),
            in_specs=[pl.BlockSpec(memory_space=pl.ANY)],
            out_specs=pl.BlockSpec((rows, width), lambda i, tok: (i, 0)),
            scratch_shapes=[pltpu.SemaphoreType.DMA]),
        compiler_params=_params(1),
        name="moe_dispatch",
    )(tok_buf, hp)


def _weight_tile_copy(w_hbm, e, t, stage, sem):
    tn = stage.shape[1]
    col = pl.multiple_of(t * tn, tn)
    return pltpu.make_async_copy(w_hbm.at[e, :, pl.ds(col, tn)], stage, sem)


def _refresh_weights(s, tile_ref, exp_ref, first_ref, next_ref, w_hbms, stages, w_bf16s, sem):
    def copies(step):
        return [_weight_tile_copy(w, exp_ref[step], tile_ref[step], st, sem.at[n])
                for n, (w, st) in enumerate(zip(w_hbms, stages))]

    @pl.when(s == 0)
    def _():
        for c in copies(0):
            c.start()

    @pl.when(first_ref[s] == 1)
    def _():
        for c in copies(s):
            c.wait()
        for st, wb in zip(stages, w_bf16s):
            wb[...] = st[...].astype(BF16)
        nxt = next_ref[s]

        @pl.when(nxt >= 0)
        def _():
            for c in copies(nxt):
                c.start()


def _moe_glu_kernel(blk_ref, tile_ref, exp_ref, live_ref, first_ref, next_ref,
                    x_ref, wg_hbm, wu_hbm, o_ref, stage_g, stage_u, wg_s, wu_s, sem):
    s = pl.program_id(0)
    _refresh_weights(s, tile_ref, exp_ref, first_ref, next_ref,
                     (wg_hbm, wu_hbm), (stage_g, stage_u), (wg_s, wu_s), sem)

    @pl.when(live_ref[s] == 1)
    def _():
        x = jnp.concatenate(_unpack_bf16_pair(x_ref[...]), axis=-1)
        p = jnp.dot(x, wg_s[...], preferred_element_type=F32)
        q = jnp.dot(x, wu_s[...], preferred_element_type=F32)
        o_ref[...] = (_silu(p) * q).astype(o_ref.dtype)

    @pl.when(live_ref[s] == 0)
    def _():
        o_ref[...] = jnp.zeros_like(o_ref)


def _moe_down_kernel(blk_ref, tile_ref, exp_ref, live_ref, first_ref, next_ref,
                     a_ref, w_hbm, o_ref, stage, w_s, sem):
    s = pl.program_id(0)
    _refresh_weights(s, tile_ref, exp_ref, first_ref, next_ref,
                     (w_hbm,), (stage,), (w_s,), sem)

    @pl.when(live_ref[s] == 1)
    def _():
        o_ref[...] = jnp.dot(a_ref[...], w_s[...], preferred_element_type=F32)

    @pl.when(live_ref[s] == 0)
    def _():
        o_ref[...] = jnp.zeros_like(o_ref)


def _grouped_schedule(block_start, block_count, used_blocks, n_blocks, n_tiles):
    steps = n_blocks * n_tiles
    step_end = (block_start + block_count) * n_tiles
    s = jnp.arange(steps, dtype=jnp.int32)
    e = jnp.minimum(jnp.sum((s[:, None] >= step_end[None, :]).astype(jnp.int32), axis=1),
                    N_EXPERTS - 1)
    local = s - block_start[e] * n_tiles
    nb = jnp.maximum(block_count[e], 1)
    tile = (local // nb).astype(jnp.int32)
    row = local % nb
    blk = (block_start[e] + row).astype(jnp.int32)
    live = (blk < used_blocks).astype(jnp.int32)
    first = (row == 0).astype(jnp.int32)
    nxt = s + nb
    nxt = jnp.where(nxt < steps, nxt, -1).astype(jnp.int32)
    return blk, tile, e, live, first, nxt


def _moe_glu(sched, xg, wg, wu, tn=512):
    cap, half = xg.shape
    _, d, n = wg.shape
    assert d == 2 * half
    steps = sched[0].shape[0]
    return pl.pallas_call(
        _moe_glu_kernel,
        out_shape=jax.ShapeDtypeStruct((cap, n), BF16),
        grid_spec=pltpu.PrefetchScalarGridSpec(
            num_scalar_prefetch=len(sched),
            grid=(steps,),
            in_specs=[
                pl.BlockSpec((MOE_BLOCK, half), lambda s, b, *_: (b[s], 0)),
                pl.BlockSpec(memory_space=pl.ANY),
                pl.BlockSpec(memory_space=pl.ANY),
            ],
            out_specs=pl.BlockSpec((MOE_BLOCK, tn), lambda s, b, t, *_: (b[s], t[s])),
            scratch_shapes=[pltpu.VMEM((d, tn), F32), pltpu.VMEM((d, tn), F32),
                            pltpu.VMEM((d, tn), BF16), pltpu.VMEM((d, tn), BF16),
                            pltpu.SemaphoreType.DMA((2,))]),
        compiler_params=_params(1),
        name="moe_glu",
    )(*sched, xg, wg, wu)


def _moe_down(sched, a, wd, tn=1024):
    cap, k = a.shape
    n = wd.shape[2]
    steps = sched[0].shape[0]
    return pl.pallas_call(
        _moe_down_kernel,
        out_shape=jax.ShapeDtypeStruct((cap, n), F32),
        grid_spec=pltpu.PrefetchScalarGridSpec(
            num_scalar_prefetch=len(sched),
            grid=(steps,),
            in_specs=[
                pl.BlockSpec((MOE_BLOCK, k), lambda s, b, *_: (b[s], 0)),
                pl.BlockSpec(memory_space=pl.ANY),
            ],
            out_specs=pl.BlockSpec((MOE_BLOCK, tn), lambda s, b, t, *_: (b[s], t[s])),
            scratch_shapes=[pltpu.VMEM((k, tn), F32), pltpu.VMEM((k, tn), BF16),
                            pltpu.SemaphoreType.DMA((1,))]),
        compiler_params=_params(1),
        name="moe_down",
    )(*sched, a, wd)


def _combine_kernel(pos_ref, x_ref, info_ref, g_ref, fg_ref, y_hbm, o_ref, buf, sem,
                    *, rows, n_tok):
    i = pl.program_id(0)
    n = pl.num_programs(0)

    def issue(blk, slot):
        def body(r, carry):
            for k in range(2):
                row = pos_ref[k * n_tok + blk * rows + r]
                _row_copy(y_hbm, row, buf.at[slot, k], r, sem.at[slot]).start()
            return carry
        lax.fori_loop(0, rows, body, 0, unroll=4)

    @pl.when(i == 0)
    def _():
        issue(0, 0)

    @pl.when(i + 1 < n)
    def _():
        issue(i + 1, (i + 1) % 2)

    slot = i % 2

    def wait_body(r, carry):
        for k in range(2):
            _row_copy(y_hbm, 0, buf.at[slot, k], r, sem.at[slot]).wait()
        return carry
    lax.fori_loop(0, rows, wait_body, 0, unroll=4)

    info = info_ref[...]
    f = info[:, 2:3] * buf[slot, 0] + info[:, 3:4] * buf[slot, 1]
    xn = x_ref[...] + g_ref[0] * f
    ms = jnp.mean(xn * xn, axis=-1, keepdims=True)
    o_ref[...] = xn * lax.rsqrt(ms + RMS_EPS) * fg_ref[...]


def _combine_final_norm(pos, x2, info, gate, final_g, y, seq, rows=256):
    t, d = x2.shape
    tpb = seq // rows
    return pl.pallas_call(
        functools.partial(_combine_kernel, rows=rows, n_tok=t),
        out_shape=jax.ShapeDtypeStruct((t, d), F32),
        grid_spec=pltpu.PrefetchScalarGridSpec(
            num_scalar_prefetch=1,
            grid=(t // rows,),
            in_specs=[
                pl.BlockSpec((rows, d), lambda i, p: (i, 0)),
                pl.BlockSpec((rows, LANES), lambda i, p: (i, 0)),
                pl.BlockSpec((1, 1, d), lambda i, p: (i // tpb, 0, 0)),
                pl.BlockSpec((1, d), lambda i, p: (0, 0)),
                pl.BlockSpec(memory_space=pl.ANY),
            ],
            out_specs=pl.BlockSpec((rows, d), lambda i, p: (i, 0)),
            scratch_shapes=[pltpu.VMEM((2, 2, rows, d), F32),
                            pltpu.SemaphoreType.DMA((2,))]),
        compiler_params=_params(1),
        name="moe_combine_final_norm",
    )(pos, x2, info, gate, final_g.reshape(1, d), y)


def _moe_layer_and_final_norm(x2, g, shift, scale, gate, router_w, wg, wu, wd,
                              final_g, seq):
    t, d = x2.shape
    rw_pad = jnp.pad(router_w, ((0, 0), (0, LANES - N_EXPERTS)))
    hp, info, cnt = _norm_route(x2, g, shift, scale, rw_pad, seq)

    e_tk = info[:, 0:2].astype(jnp.int32)
    rank_tk = info[:, 4:6].astype(jnp.int32)
    n_assign = e_tk.size
    counts = cnt[0, :N_EXPERTS].astype(jnp.int32)
    padded = (counts + MOE_BLOCK - 1) // MOE_BLOCK * MOE_BLOCK
    pend = jnp.cumsum(padded)
    pstart = pend - padded
    seg_start = jnp.sum(
        jnp.where(e_tk[..., None] == jnp.arange(N_EXPERTS, dtype=jnp.int32), pstart, 0), axis=-1)
    dest = (seg_start + rank_tk).astype(jnp.int32).reshape(-1)
    n_blocks = -(-n_assign // MOE_BLOCK) + N_EXPERTS
    cap = n_blocks * MOE_BLOCK
    tok_flat = jnp.arange(n_assign, dtype=jnp.int32) // 2
    tok_buf = jnp.full((cap,), t - 1, jnp.int32).at[dest].set(tok_flat)

    block_start = (pstart // MOE_BLOCK).astype(jnp.int32)
    block_count = (padded // MOE_BLOCK).astype(jnp.int32)
    used_blocks = pend[-1] // MOE_BLOCK
    block_count = block_count.at[N_EXPERTS - 1].set(n_blocks - block_start[N_EXPERTS - 1])

    xg = _dispatch(tok_buf, hp)

    def sched(n_tiles):
        return _grouped_schedule(block_start, block_count, used_blocks, n_blocks, n_tiles)

    tn_glu, tn_down = 512, 1024
    hmid = _moe_glu(sched(wg.shape[2] // tn_glu), xg, wg, wu, tn=tn_glu)
    y = _moe_down(sched(wd.shape[2] // tn_down), hmid, wd, tn=tn_down)

    pos = dest.reshape(t, 2).T.reshape(-1)
    return _combine_final_norm(pos, x2, info, gate, final_g, y, seq)


def kernel(x, c, norm_mix_g, norm_ffn_g, w_ada, b_ada, w_in, w_out, sc_conv_w, cf_conv_w,
           cf_conv_b, cf_ln_g, cf_ln_b, ffn_w_gate, ffn_w_up, ffn_w_down, router_w,
           moe_w_gate, moe_w_up, moe_w_down, final_g):
    bsz, seq, d = x.shape
    depth = w_ada.shape[0]
    assert depth == 2 and d == D_MODEL
    t = bsz * seq
    x2 = x.reshape(t, d)

    c_pad = jnp.pad(c, ((0, MOD_ROWS - bsz), (0, 0)))
    mod = _adaln_mod(c_pad, w_ada, b_ada)
    mod = mod[:, :bsz].reshape(depth, bsz, N_MOD, 1, d)

    ffn_wg_b = ffn_w_gate.astype(BF16)
    ffn_wu_b = ffn_w_up.astype(BF16)
    ffn_wd_b = ffn_w_down.astype(BF16)

    out = None
    for l in range(depth):
        sh1, sc1, g1, sh2, sc2, g2 = (mod[l, :, m] for m in range(N_MOD))

        h = _norm_modulate(x2, norm_mix_g[l], sh1, sc1, seq)
        y_sc, conv = _proj_mixers(h, w_in, l, sc_conv_w[l], cf_conv_w[l],
                                  cf_conv_b[l], seq)
        y_cf = _ln_silu(conv, cf_ln_g[l], cf_ln_b[l])
        x2 = _out_proj_residual(y_sc, y_cf, w_out, l, x2, g1, seq)

        i = l // 2
        if l % 2 == 0:
            h = _norm_modulate(x2, norm_ffn_g[l], sh2, sc2, seq)
            hmid = _glu(h, ffn_wg_b, ffn_wu_b, i)
            x2 = _down_residual(hmid, ffn_wd_b, i, x2, g2, seq)
        else:
            out = _moe_layer_and_final_norm(
                x2, norm_ffn_g[l], sh2, sc2, g2, router_w[i],
                moe_w_gate[i], moe_w_up[i], moe_w_down[i], final_g, seq)
    return out.reshape(bsz, seq, d)
```

```python
import functools

import jax
import jax.numpy as jnp
from jax import lax
from jax.experimental import pallas as pl
from jax.experimental.pallas import tpu as pltpu

F32 = jnp.float32
BF16 = jnp.bfloat16

D_MODEL = 4096
D_SC = 2048
D_CF = 2048
D_IN = 3 * D_SC + 2 * D_CF
SC_WIDTH = 3
CF_WIDTH = 31
N_EXPERTS = 8
D_FF_EXPERT = 4096
MOE_BLOCK = 512
N_MOD = 6
RMS_EPS = 1e-6
LN_EPS = 1e-5

V7X_VMEM_LIMIT = 56 * 1024 * 1024
LANES = 128
SUBLANES = 8
MOD_ROWS = 8


def _params(n_axes, vmem=V7X_VMEM_LIMIT, flags=None):
    return pltpu.CompilerParams(
        dimension_semantics=("arbitrary",) * n_axes, vmem_limit_bytes=vmem, flags=flags)


def _silu(v):
    return v * jax.nn.sigmoid(v)


def _mod_kernel(c_ref, w_ref, b_ref, o_ref):
    c = c_ref[...]
    acc = jnp.dot(_silu(c).astype(BF16), w_ref[0].astype(BF16),
                  preferred_element_type=F32)
    o_ref[0] = acc + b_ref[0]


def _adaln_mod(c_pad, w_ada, b_ada, tn=512):
    depth, d, n = w_ada.shape
    return pl.pallas_call(
        _mod_kernel,
        out_shape=jax.ShapeDtypeStruct((depth, MOD_ROWS, n), F32),
        grid=(depth, n // tn),
        in_specs=[
            pl.BlockSpec((MOD_ROWS, d), lambda l, j: (0, 0)),
            pl.BlockSpec((1, d, tn), lambda l, j: (l, 0, j)),
            pl.BlockSpec((1, 1, tn), lambda l, j: (l, 0, j)),
        ],
        out_specs=pl.BlockSpec((1, MOD_ROWS, tn), lambda l, j: (l, 0, j)),
        compiler_params=_params(2),
        name="adaln_mod",
    )(c_pad, w_ada, b_ada.reshape(depth, 1, n))


def _norm_mod(x, g_ref, sh_ref, sc_ref):
    ms = jnp.mean(x * x, axis=-1, keepdims=True)
    y = x * lax.rsqrt(ms + RMS_EPS) * g_ref[...]
    return y * (1.0 + sc_ref[0]) + sh_ref[0]


def _norm_mod_kernel(x_ref, g_ref, sh_ref, sc_ref, o_ref):
    o_ref[...] = _norm_mod(x_ref[...], g_ref, sh_ref, sc_ref).astype(o_ref.dtype)


def _pack_bf16_pair(lo, hi):
    def rounded_bits(v):
        b = pltpu.bitcast(v, jnp.uint32)
        return b + jnp.uint32(0x7FFF) + ((b >> 16) & jnp.uint32(1))
    return (rounded_bits(hi) & jnp.uint32(0xFFFF0000)) | (rounded_bits(lo) >> 16)


def _unpack_bf16_pair(w):
    lo = pltpu.bitcast(w << 16, F32).astype(BF16)
    hi = pltpu.bitcast(w & jnp.uint32(0xFFFF0000), F32).astype(BF16)
    return lo, hi


def _norm_route_kernel(x_ref, g_ref, sh_ref, sc_ref, rw_ref, hp_ref, info_ref, cnt_ref):
    @pl.when(pl.program_id(0) == 0)
    def _():
        cnt_ref[...] = jnp.zeros(cnt_ref.shape, F32)

    h = _norm_mod(x_ref[...], g_ref, sh_ref, sc_ref)
    half = h.shape[1] // 2
    hp_ref[...] = _pack_bf16_pair(h[:, :half], h[:, half:])
    rw = rw_ref[...]
    h_hi = h.astype(BF16)
    h_lo = (h - h_hi.astype(F32)).astype(BF16)
    rw_hi = rw.astype(BF16)
    rw_lo = (rw - rw_hi.astype(F32)).astype(BF16)
    logits = (jnp.dot(h_hi, rw_hi, preferred_element_type=F32)
              + jnp.dot(h_lo, rw_hi, preferred_element_type=F32)
              + jnp.dot(h_hi, rw_lo, preferred_element_type=F32))
    lane = lax.broadcasted_iota(jnp.int32, logits.shape, 1)
    neg = jnp.float32(-jnp.inf)
    lg = jnp.where(lane < N_EXPERTS, logits, neg)
    m1 = jnp.max(lg, axis=-1, keepdims=True)
    i1 = jnp.min(jnp.where(lg == m1, lane, LANES), axis=-1, keepdims=True)
    lg2 = jnp.where(lane == i1, neg, lg)
    m2 = jnp.max(lg2, axis=-1, keepdims=True)
    i2 = jnp.min(jnp.where(lg2 == m2, lane, LANES), axis=-1, keepdims=True)
    e2 = jnp.exp(m2 - m1)
    den = 1.0 + e2

    oh1 = (lane == i1).astype(F32)
    oh2 = (lane == i2).astype(F32)
    oh = oh1 + oh2
    tm = oh.shape[0]
    tri = (lax.broadcasted_iota(jnp.int32, (tm, tm), 1)
           < lax.broadcasted_iota(jnp.int32, (tm, tm), 0)).astype(BF16)
    before = jnp.dot(tri, oh.astype(BF16), preferred_element_type=F32) + cnt_ref[...]
    rank1 = jnp.sum(oh1 * before, axis=-1, keepdims=True)
    rank2 = jnp.sum(oh2 * before, axis=-1, keepdims=True)
    cnt_ref[...] = cnt_ref[...] + jnp.sum(oh, axis=0, keepdims=True)

    info = jnp.where(lane == 0, i1.astype(F32),
           jnp.where(lane == 1, i2.astype(F32),
           jnp.where(lane == 2, 1.0 / den,
           jnp.where(lane == 3, e2 / den,
           jnp.where(lane == 4, rank1,
           jnp.where(lane == 5, rank2, 0.0))))))
    info_ref[...] = info


def _row_specs(tm, d, tiles_per_batch):
    return [
        pl.BlockSpec((tm, d), lambda i: (i, 0)),
        pl.BlockSpec((1, d), lambda i: (0, 0)),
        pl.BlockSpec((1, 1, d), lambda i: (i // tiles_per_batch, 0, 0)),
        pl.BlockSpec((1, 1, d), lambda i: (i // tiles_per_batch, 0, 0)),
    ]


def _norm_modulate(x2, g, shift, scale, seq, tm=512):
    t, d = x2.shape
    return pl.pallas_call(
        _norm_mod_kernel,
        out_shape=jax.ShapeDtypeStruct((t, d), BF16),
        grid=(t // tm,),
        in_specs=_row_specs(tm, d, seq // tm),
        out_specs=pl.BlockSpec((tm, d), lambda i: (i, 0)),
        compiler_params=_params(1),
        name="norm_modulate",
    )(x2, g.reshape(1, d), shift, scale)


def _norm_route(x2, g, shift, scale, rw_pad, seq, tm=256):
    t, d = x2.shape
    return pl.pallas_call(
        _norm_route_kernel,
        out_shape=(jax.ShapeDtypeStruct((t, d // 2), jnp.uint32),
                   jax.ShapeDtypeStruct((t, LANES), F32),
                   jax.ShapeDtypeStruct((1, LANES), F32)),
        grid=(t // tm,),
        in_specs=_row_specs(tm, d, seq // tm) + [
            pl.BlockSpec((d, LANES), lambda i: (0, 0))],
        out_specs=(pl.BlockSpec((tm, d // 2), lambda i: (i, 0)),
                   pl.BlockSpec((tm, LANES), lambda i: (i, 0)),
                   pl.BlockSpec((1, LANES), lambda i: (0, 0))),
        compiler_params=_params(1),
        name="norm_route",
    )(x2, g.reshape(1, d), shift, scale, rw_pad)


def _out_proj_kernel(a1_ref, a2_ref, w_hbm, x_ref, g_ref, o_ref, stage_s, w_s, sem, *, layer):
    j = pl.program_id(0)
    i = pl.program_id(1)
    tn = stage_s.shape[1]

    def weight_copy(tile):
        col = pl.multiple_of(tile * tn, tn)
        return pltpu.make_async_copy(w_hbm.at[layer, :, pl.ds(col, tn)], stage_s, sem)

    @pl.when((j == 0) & (i == 0))
    def _():
        weight_copy(0).start()

    @pl.when(i == 0)
    def _():
        weight_copy(j).wait()
        w_s[...] = stage_s[...].astype(BF16)

        @pl.when(j + 1 < pl.num_programs(0))
        def _():
            weight_copy(j + 1).start()

    k1 = a1_ref.shape[1]
    acc = jnp.dot(a1_ref[...], w_s[:k1, :], preferred_element_type=F32)
    acc = acc + jnp.dot(a2_ref[...], w_s[k1:, :], preferred_element_type=F32)
    o_ref[...] = x_ref[...] + g_ref[0] * acc


def _out_proj_residual(y_sc, y_cf, w_out, layer, x2, gate, seq, tm=1024, tn=512):
    m, k1 = y_sc.shape
    k2 = y_cf.shape[1]
    n = w_out.shape[2]
    assert k1 + k2 == w_out.shape[1]
    tpb = seq // tm
    return pl.pallas_call(
        functools.partial(_out_proj_kernel, layer=layer),
        out_shape=jax.ShapeDtypeStruct((m, n), F32),
        grid=(n // tn, m // tm),
        in_specs=[pl.BlockSpec((tm, k1), lambda j, i: (i, 0)),
                  pl.BlockSpec((tm, k2), lambda j, i: (i, 0)),
                  pl.BlockSpec(memory_space=pl.ANY),
                  pl.BlockSpec((tm, tn), lambda j, i: (i, j)),
                  pl.BlockSpec((1, 1, tn), lambda j, i: (i // tpb, 0, j))],
        out_specs=pl.BlockSpec((tm, tn), lambda j, i: (i, j)),
        scratch_shapes=[pltpu.VMEM((k1 + k2, tn), F32),
                        pltpu.VMEM((k1 + k2, tn), BF16),
                        pltpu.SemaphoreType.DMA],
        compiler_params=_params(2),
        name="out_proj_residual",
    )(y_sc, y_cf, w_out, x2, gate)


def _glu_kernel(a_ref, wg_ref, wu_ref, o_ref):
    a = a_ref[...]
    p = jnp.dot(a, wg_ref[0], preferred_element_type=F32)
    q = jnp.dot(a, wu_ref[0], preferred_element_type=F32)
    o_ref[...] = (_silu(p) * q).astype(o_ref.dtype)


def _glu(a, wg, wu, layer, tm=2048, tn=256):
    m, k = a.shape
    n = wg.shape[2]
    assert n % tn == 0
    return pl.pallas_call(
        _glu_kernel,
        out_shape=jax.ShapeDtypeStruct((m, n), BF16),
        grid=(m // tm, n // tn),
        in_specs=[pl.BlockSpec((tm, k), lambda i, j: (i, 0)),
                  pl.BlockSpec((1, k, tn), lambda i, j: (layer, 0, j)),
                  pl.BlockSpec((1, k, tn), lambda i, j: (layer, 0, j))],
        out_specs=pl.BlockSpec((tm, tn), lambda i, j: (i, j)),
        compiler_params=_params(2),
        name="glu",
    )(a, wg, wu)


def _down_residual_kernel(a_ref, w_ref, x_ref, g_ref, o_ref):
    acc = jnp.dot(a_ref[...], w_ref[0], preferred_element_type=F32)
    o_ref[...] = x_ref[...] + g_ref[0] * acc


def _down_residual(a, w, layer, x2, gate, seq, tm=512, tn=512):
    m, k = a.shape
    n = w.shape[2]
    tpb = seq // tm
    return pl.pallas_call(
        _down_residual_kernel,
        out_shape=jax.ShapeDtypeStruct((m, n), F32),
        grid=(m // tm, n // tn),
        in_specs=[pl.BlockSpec((tm, k), lambda i, j: (i, 0)),
                  pl.BlockSpec((1, k, tn), lambda i, j: (layer, 0, j)),
                  pl.BlockSpec((tm, tn), lambda i, j: (i, j)),
                  pl.BlockSpec((1, 1, tn), lambda i, j: (i // tpb, 0, j))],
        out_specs=pl.BlockSpec((tm, tn), lambda i, j: (i, j)),
        compiler_params=_params(2),
        name="down_residual",
    )(a, w, x2, gate)


SC_HALO = SUBLANES
CF_HALO = 32
MIX_TC = 256


N_PROJ = 5


def _proj_mixer_kernel(h_ref, w_hbm, scw_ref, cfw_ref, cfb_ref, ysc_ref, conv_ref,
                       stage_s, w_s, cv_s, u_s, sh_s, sem, *, layer, tm, tiles_per_seq):
    j = pl.program_id(0)
    i = pl.program_id(1)
    n_chunks = pl.num_programs(0)

    def weight_copies(chunk):
        return [pltpu.make_async_copy(
            w_hbm.at[layer, :, pl.ds(pl.multiple_of((n * n_chunks + chunk) * MIX_TC, MIX_TC),
                                     MIX_TC)],
            stage_s.at[n], sem.at[n]) for n in range(N_PROJ)]

    @pl.when((j == 0) & (i == 0))
    def _():
        for c in weight_copies(0):
            c.start()

    @pl.when(i == 0)
    def _():
        for c in weight_copies(j):
            c.wait()
        for n in range(N_PROJ):
            w_s[n] = stage_s[n].astype(BF16)

        @pl.when(j + 1 < n_chunks)
        def _():
            for c in weight_copies(j + 1):
                c.start()

    seq_start = (i % tiles_per_seq) == 0
    h = h_ref[...]

    def proj(n):
        return jnp.dot(h, w_s[n], preferred_element_type=F32)

    @pl.when(seq_start)
    def _():
        u_s[0:CF_HALO, :] = jnp.zeros((CF_HALO, MIX_TC), F32)
        cv_s[0:SC_HALO, :] = jnp.zeros((SC_HALO, MIX_TC), F32)

    u_s[CF_HALO:, :] = proj(3) * jax.nn.sigmoid(proj(4))
    cv_s[SC_HALO:, :] = proj(1) * proj(2)
    b_gate = proj(0)
    span = tm + CF_HALO - SUBLANES
    for r in range(1, SUBLANES):
        sh_s[r - 1] = u_s[r:r + span, :]
    cfw = cfw_ref[...]
    acc = jnp.broadcast_to(cfb_ref[...], (tm, MIX_TC))
    base = CF_HALO - (CF_WIDTH - 1)
    for k in range(CF_WIDTH):
        q, r = divmod(base + k, SUBLANES)
        if r == 0:
            tap = u_s[q * SUBLANES:q * SUBLANES + tm, :]
        else:
            tap = sh_s[r - 1, q * SUBLANES:q * SUBLANES + tm, :]
        acc = acc + cfw[k:k + 1] * tap
    conv_ref[...] = acc
    u_s[0:CF_HALO, :] = u_s[tm:tm + CF_HALO, :]

    scw = scw_ref[...]
    y = scw[0:1] * cv_s[SC_HALO - 2:SC_HALO - 2 + tm, :]
    y = y + scw[1:2] * cv_s[SC_HALO - 1:SC_HALO - 1 + tm, :]
    y = y + scw[2:3] * cv_s[SC_HALO:SC_HALO + tm, :]
    ysc_ref[...] = (b_gate * y).astype(ysc_ref.dtype)
    cv_s[0:SC_HALO, :] = cv_s[tm:tm + SC_HALO, :]


def _ln_silu_kernel(u_ref, g_ref, b_ref, o_ref):
    u = u_ref[...]
    mu = jnp.mean(u, axis=-1, keepdims=True)
    uc = u - mu
    var = jnp.mean(uc * uc, axis=-1, keepdims=True)
    yn = uc * lax.rsqrt(var + LN_EPS) * g_ref[...] + b_ref[...]
    o_ref[...] = _silu(yn).astype(o_ref.dtype)


def _proj_mixers(h, w_in, layer, sc_w, cf_w, cf_b, seq, tm=512):
    t, d = h.shape
    tc = MIX_TC
    nch = D_SC // tc
    assert D_SC == D_CF and w_in.shape[2] == N_PROJ * D_SC and seq % tm == 0

    kern = functools.partial(_proj_mixer_kernel, layer=layer, tm=tm,
                             tiles_per_seq=seq // tm)
    return pl.pallas_call(
        kern,
        out_shape=(jax.ShapeDtypeStruct((t, D_SC), BF16),
                   jax.ShapeDtypeStruct((t, D_CF), F32)),
        grid=(nch, t // tm),
        in_specs=[
            pl.BlockSpec((tm, d), lambda j, i: (i, 0)),
            pl.BlockSpec(memory_space=pl.ANY),
            pl.BlockSpec((SC_WIDTH, tc), lambda j, i: (0, j)),
            pl.BlockSpec((CF_WIDTH, tc), lambda j, i: (0, j)),
            pl.BlockSpec((1, tc), lambda j, i: (0, j)),
        ],
        out_specs=(pl.BlockSpec((tm, tc), lambda j, i: (i, j)),
                   pl.BlockSpec((tm, tc), lambda j, i: (i, j))),
        scratch_shapes=[pltpu.VMEM((N_PROJ, d, tc), F32),
                        pltpu.VMEM((N_PROJ, d, tc), BF16),
                        pltpu.VMEM((tm + SC_HALO, tc), F32),
                        pltpu.VMEM((tm + CF_HALO, tc), F32),
                        pltpu.VMEM((SUBLANES - 1, tm + CF_HALO - SUBLANES, tc), F32),
                        pltpu.SemaphoreType.DMA((N_PROJ,))],
        compiler_params=_params(2, vmem=62 * 1024 * 1024),
        name="proj_mixers",
    )(h, w_in, sc_w, cf_w, cf_b.reshape(1, D_CF))


def _ln_silu(u, ln_g, ln_b, tm=512):
    t, n = u.shape
    return pl.pallas_call(
        _ln_silu_kernel,
        out_shape=jax.ShapeDtypeStruct((t, n), BF16),
        grid=(t // tm,),
        in_specs=[pl.BlockSpec((tm, n), lambda i: (i, 0)),
                  pl.BlockSpec((1, n), lambda i: (0, 0)),
                  pl.BlockSpec((1, n), lambda i: (0, 0))],
        out_specs=pl.BlockSpec((tm, n), lambda i: (i, 0)),
        compiler_params=_params(1),
        name="ln_silu",
    )(u, ln_g.reshape(1, n), ln_b.reshape(1, n))


def _row_copy(src_hbm, row, dst, r, sem):
    return pltpu.make_async_copy(src_hbm.at[pl.ds(row, 1)], dst.at[pl.ds(r, 1)], sem)


N_DMA_PRIORITIES = 2


def _dispatch_kernel(tok_ref, src_hbm, o_ref, sem, *, rows):
    base = pl.program_id(0) * rows

    def issue(g, carry):
        for p in range(N_DMA_PRIORITIES):
            r = g * N_DMA_PRIORITIES + p
            _row_copy(src_hbm, tok_ref[base + r], o_ref, r, sem).start(priority=p)
        return carry
    lax.fori_loop(0, rows // N_DMA_PRIORITIES, issue, 0, unroll=4)

    def drain(r, carry):
        _row_copy(src_hbm, 0, o_ref, r, sem).wait()
        return carry
    lax.fori_loop(0, rows, drain, 0, unroll=8)


def _dispatch(tok_buf, hp, rows=MOE_BLOCK):
    cap = tok_buf.shape[0]
    width = hp.shape[1]
    return pl.pallas_call(
        functools.partial(_dispatch_kernel, rows=rows),
        out_shape=jax.ShapeDtypeStruct((cap, width), hp.dtype),
        grid_spec=pltpu.PrefetchScalarGridSpec(
            num_scalar_prefetch=1,
            grid=(cap // rows,),
            in_specs=[pl.BlockSpec(memory_space=pl.ANY)],
            out_specs=pl.BlockSpec((rows, width), lambda i, tok: (i, 0)),
            scratch_shapes=[pltpu.SemaphoreType.DMA]),
        compiler_params=_params(1),
        name="moe_dispatch",
    )(tok_buf, hp)


ROW_BLOCK_BUFFERS = 3


def _weight_tile_copy(w_hbm, e, t, stage, sem):
    tn = stage.shape[1]
    col = pl.multiple_of(t * tn, tn)
    return pltpu.make_async_copy(w_hbm.at[e, :, pl.ds(col, tn)], stage, sem)


def _refresh_weights(s, tile_ref, exp_ref, first_ref, next_ref, w_hbms, stages, w_bf16s, sem):
    def copies(step):
        return [_weight_tile_copy(w, exp_ref[step], tile_ref[step], st, sem.at[n])
                for n, (w, st) in enumerate(zip(w_hbms, stages))]

    @pl.when(s == 0)
    def _():
        for c in copies(0):
            c.start()

    @pl.when(first_ref[s] == 1)
    def _():
        for c in copies(s):
            c.wait()
        for st, wb in zip(stages, w_bf16s):
            wb[...] = st[...].astype(BF16)
        nxt = next_ref[s]

        @pl.when(nxt >= 0)
        def _():
            for c in copies(nxt):
                c.start()


def _moe_glu_kernel(blk_ref, tile_ref, exp_ref, live_ref, first_ref, next_ref,
                    x_hbm, wg_hbm, wu_hbm, o_ref, stage_g, stage_u, wg_s, wu_s, x_s, sem, x_sem):
    s = pl.program_id(0)
    n_steps = pl.num_programs(0)
    _refresh_weights(s, tile_ref, exp_ref, first_ref, next_ref,
                     (wg_hbm, wu_hbm), (stage_g, stage_u), (wg_s, wu_s), sem)

    def row_copy(step):
        slot = step % ROW_BLOCK_BUFFERS
        row0 = pl.multiple_of(blk_ref[step] * MOE_BLOCK, MOE_BLOCK)
        return pltpu.make_async_copy(x_hbm.at[pl.ds(row0, MOE_BLOCK)], x_s.at[slot],
                                     x_sem.at[slot])

    @pl.when(s == 0)
    def _():
        for ahead in range(ROW_BLOCK_BUFFERS - 1):
            row_copy(ahead).start()

    row_copy(s).wait()

    @pl.when(s + ROW_BLOCK_BUFFERS - 1 < n_steps)
    def _():
        row_copy(s + ROW_BLOCK_BUFFERS - 1).start()

    @pl.when(live_ref[s] == 1)
    def _():
        x = jnp.concatenate(_unpack_bf16_pair(x_s[s % ROW_BLOCK_BUFFERS]), axis=-1)
        p = jnp.dot(x, wg_s[...], preferred_element_type=F32)
        q = jnp.dot(x, wu_s[...], preferred_element_type=F32)
        o_ref[...] = (_silu(p) * q).astype(o_ref.dtype)

    @pl.when(live_ref[s] == 0)
    def _():
        o_ref[...] = jnp.zeros_like(o_ref)


def _moe_down_kernel(blk_ref, tile_ref, exp_ref, live_ref, first_ref, next_ref,
                     a_ref, w_hbm, o_ref, stage, w_s, sem):
    s = pl.program_id(0)
    _refresh_weights(s, tile_ref, exp_ref, first_ref, next_ref,
                     (w_hbm,), (stage,), (w_s,), sem)

    @pl.when(live_ref[s] == 1)
    def _():
        o_ref[...] = jnp.dot(a_ref[...], w_s[...], preferred_element_type=F32)

    @pl.when(live_ref[s] == 0)
    def _():
        o_ref[...] = jnp.zeros_like(o_ref)


def _grouped_schedule(block_start, block_count, used_blocks, n_blocks, n_tiles):
    steps = n_blocks * n_tiles
    step_end = (block_start + block_count) * n_tiles
    s = jnp.arange(steps, dtype=jnp.int32)
    e = jnp.minimum(jnp.sum((s[:, None] >= step_end[None, :]).astype(jnp.int32), axis=1),
                    N_EXPERTS - 1)
    local = s - block_start[e] * n_tiles
    nb = jnp.maximum(block_count[e], 1)
    tile = (local // nb).astype(jnp.int32)
    row = local % nb
    blk = (block_start[e] + row).astype(jnp.int32)
    live = (blk < used_blocks).astype(jnp.int32)
    first = (row == 0).astype(jnp.int32)
    nxt = s + nb
    nxt = jnp.where(nxt < steps, nxt, -1).astype(jnp.int32)
    return blk, tile, e, live, first, nxt


def _moe_glu(sched, xg, wg, wu, tn=512):
    cap, half = xg.shape
    _, d, n = wg.shape
    assert d == 2 * half
    steps = sched[0].shape[0]
    return pl.pallas_call(
        _moe_glu_kernel,
        out_shape=jax.ShapeDtypeStruct((cap, n), BF16),
        grid_spec=pltpu.PrefetchScalarGridSpec(
            num_scalar_prefetch=len(sched),
            grid=(steps,),
            in_specs=[
                pl.BlockSpec(memory_space=pl.ANY),
                pl.BlockSpec(memory_space=pl.ANY),
                pl.BlockSpec(memory_space=pl.ANY),
            ],
            out_specs=pl.BlockSpec((MOE_BLOCK, tn), lambda s, b, t, *_: (b[s], t[s])),
            scratch_shapes=[pltpu.VMEM((d, tn), F32), pltpu.VMEM((d, tn), F32),
                            pltpu.VMEM((d, tn), BF16), pltpu.VMEM((d, tn), BF16),
                            pltpu.VMEM((ROW_BLOCK_BUFFERS, MOE_BLOCK, half), xg.dtype),
                            pltpu.SemaphoreType.DMA((2,)),
                            pltpu.SemaphoreType.DMA((ROW_BLOCK_BUFFERS,))]),
        compiler_params=_params(1),
        name="moe_glu",
    )(*sched, xg, wg, wu)


def _moe_down(sched, a, wd, tn=1024):
    cap, k = a.shape
    n = wd.shape[2]
    steps = sched[0].shape[0]
    return pl.pallas_call(
        _moe_down_kernel,
        out_shape=jax.ShapeDtypeStruct((cap, n), F32),
        grid_spec=pltpu.PrefetchScalarGridSpec(
            num_scalar_prefetch=len(sched),
            grid=(steps,),
            in_specs=[
                pl.BlockSpec((MOE_BLOCK, k), lambda s, b, *_: (b[s], 0)),
                pl.BlockSpec(memory_space=pl.ANY),
            ],
            out_specs=pl.BlockSpec((MOE_BLOCK, tn), lambda s, b, t, *_: (b[s], t[s])),
            scratch_shapes=[pltpu.VMEM((k, tn), F32), pltpu.VMEM((k, tn), BF16),
                            pltpu.SemaphoreType.DMA((1,))]),
        compiler_params=_params(1),
        name="moe_down",
    )(*sched, a, wd)


def _combine_kernel(pos_ref, x_ref, info_ref, g_ref, fg_ref, y_hbm, o_ref, buf, sem,
                    *, rows, n_tok):
    i = pl.program_id(0)
    n = pl.num_programs(0)

    def issue(blk, slot):
        def body(r, carry):
            for k in range(2):
                row = pos_ref[k * n_tok + blk * rows + r]
                _row_copy(y_hbm, row, buf.at[slot, k], r, sem.at[slot]).start()
            return carry
        lax.fori_loop(0, rows, body, 0, unroll=4)

    @pl.when(i == 0)
    def _():
        issue(0, 0)

    @pl.when(i + 1 < n)
    def _():
        issue(i + 1, (i + 1) % 2)

    slot = i % 2

    def wait_body(r, carry):
        for k in range(2):
            _row_copy(y_hbm, 0, buf.at[slot, k], r, sem.at[slot]).wait()
        return carry
    lax.fori_loop(0, rows, wait_body, 0, unroll=4)

    info = info_ref[...]
    f = info[:, 2:3] * buf[slot, 0] + info[:, 3:4] * buf[slot, 1]
    xn = x_ref[...] + g_ref[0] * f
    ms = jnp.mean(xn * xn, axis=-1, keepdims=True)
    o_ref[...] = xn * lax.rsqrt(ms + RMS_EPS) * fg_ref[...]


def _combine_final_norm(pos, x2, info, gate, final_g, y, seq, rows=256):
    t, d = x2.shape
    tpb = seq // rows
    return pl.pallas_call(
        functools.partial(_combine_kernel, rows=rows, n_tok=t),
        out_shape=jax.ShapeDtypeStruct((t, d), F32),
        grid_spec=pltpu.PrefetchScalarGridSpec(
            num_scalar_prefetch=1,
            grid=(t // rows,),
            in_specs=[
                pl.BlockSpec((rows, d), lambda i, p: (i, 0)),
                pl.BlockSpec((rows, LANES), lambda i, p: (i, 0)),
                pl.BlockSpec((1, 1, d), lambda i, p: (i // tpb, 0, 0)),
                pl.BlockSpec((1, d), lambda i, p: (0, 0)),
                pl.BlockSpec(memory_space=pl.ANY),
            ],
            out_specs=pl.BlockSpec((rows, d), lambda i, p: (i, 0)),
            scratch_shapes=[pltpu.VMEM((2, 2, rows, d), F32),
                            pltpu.SemaphoreType.DMA((2,))]),
        compiler_params=_params(1),
        name="moe_combine_final_norm",
    )(pos, x2, info, gate, final_g.reshape(1, d), y)


def _moe_layer_and_final_norm(x2, g, shift, scale, gate, router_w, wg, wu, wd,
                              final_g, seq):
    t, d = x2.shape
    rw_pad = jnp.pad(router_w, ((0, 0), (0, LANES - N_EXPERTS)))
    hp, info, cnt = _norm_route(x2, g, shift, scale, rw_pad, seq)

    e_tk = info[:, 0:2].astype(jnp.int32)
    rank_tk = info[:, 4:6].astype(jnp.int32)
    n_assign = e_tk.size
    counts = cnt[0, :N_EXPERTS].astype(jnp.int32)
    padded = (counts + MOE_BLOCK - 1) // MOE_BLOCK * MOE_BLOCK
    pend = jnp.cumsum(padded)
    pstart = pend - padded
    seg_start = jnp.sum(
        jnp.where(e_tk[..., None] == jnp.arange(N_EXPERTS, dtype=jnp.int32), pstart, 0), axis=-1)
    dest = (seg_start + rank_tk).astype(jnp.int32).reshape(-1)
    n_blocks = -(-n_assign // MOE_BLOCK) + N_EXPERTS
    cap = n_blocks * MOE_BLOCK
    tok_flat = jnp.arange(n_assign, dtype=jnp.int32) // 2
    tok_buf = jnp.full((cap,), t - 1, jnp.int32).at[dest].set(tok_flat)

    block_start = (pstart // MOE_BLOCK).astype(jnp.int32)
    block_count = (padded // MOE_BLOCK).astype(jnp.int32)
    used_blocks = pend[-1] // MOE_BLOCK
    block_count = block_count.at[N_EXPERTS - 1].set(n_blocks - block_start[N_EXPERTS - 1])

    xg = _dispatch(tok_buf, hp)

    def sched(n_tiles):
        return _grouped_schedule(block_start, block_count, used_blocks, n_blocks, n_tiles)

    tn_glu, tn_down = 512, 1024
    hmid = _moe_glu(sched(wg.shape[2] // tn_glu), xg, wg, wu, tn=tn_glu)
    y = _moe_down(sched(wd.shape[2] // tn_down), hmid, wd, tn=tn_down)

    pos = dest.reshape(t, 2).T.reshape(-1)
    return _combine_final_norm(pos, x2, info, gate, final_g, y, seq)


def kernel(x, c, norm_mix_g, norm_ffn_g, w_ada, b_ada, w_in, w_out, sc_conv_w, cf_conv_w,
           cf_conv_b, cf_ln_g, cf_ln_b, ffn_w_gate, ffn_w_up, ffn_w_down, router_w,
           moe_w_gate, moe_w_up, moe_w_down, final_g):
    bsz, seq, d = x.shape
    depth = w_ada.shape[0]
    assert depth == 2 and d == D_MODEL
    t = bsz * seq
    x2 = x.reshape(t, d)

    c_pad = jnp.pad(c, ((0, MOD_ROWS - bsz), (0, 0)))
    mod = _adaln_mod(c_pad, w_ada, b_ada)
    mod = mod[:, :bsz].reshape(depth, bsz, N_MOD, 1, d)

    ffn_wg_b = ffn_w_gate.astype(BF16)
    ffn_wu_b = ffn_w_up.astype(BF16)
    ffn_wd_b = ffn_w_down.astype(BF16)

    out = None
    for l in range(depth):
        sh1, sc1, g1, sh2, sc2, g2 = (mod[l, :, m] for m in range(N_MOD))

        h = _norm_modulate(x2, norm_mix_g[l], sh1, sc1, seq)
        y_sc, conv = _proj_mixers(h, w_in, l, sc_conv_w[l], cf_conv_w[l],
                                  cf_conv_b[l], seq)
        y_cf = _ln_silu(conv, cf_ln_g[l], cf_ln_b[l])
        x2 = _out_proj_residual(y_sc, y_cf, w_out, l, x2, g1, seq)

        i = l // 2
        if l % 2 == 0:
            h = _norm_modulate(x2, norm_ffn_g[l], sh2, sc2, seq)
            hmid = _glu(h, ffn_wg_b, ffn_wu_b, i)
            x2 = _down_residual(hmid, ffn_wd_b, i, x2, g2, seq)
        else:
            out = _moe_layer_and_final_norm(
                x2, norm_ffn_g[l], sh2, sc2, g2, router_w[i],
                moe_w_gate[i], moe_w_up[i], moe_w_down[i], final_g, seq)
    return out.reshape(bsz, seq, d)
```
